```python
import math, functools
import jax, jax.numpy as jnp
from jax import lax
import numpy as np

D_MODEL = 1024
BATCH = 4
SEQ = 8192
DEPTH = 1
DEC_BATCH = 128
DEC_SEQ = 8
PAST_LEN = 8192
PAGE_SIZE = 128

ATT_WIDTH = D_MODEL // 2
SSM_WIDTH = D_MODEL - ATT_WIDTH
HEAD_DIM = 64
N_HEADS = ATT_WIDTH // HEAD_DIM
SSM_GROUP = 16
N_SSM_GROUPS = SSM_WIDTH // SSM_GROUP
STATE_DIM = 64
D_FF = ((8 * D_MODEL + 768 - 1) // 768) * 256
IN_COLS = 3 * ATT_WIDTH + N_HEADS + SSM_WIDTH
BLOCK_Q = 128
EPS = 1e-6
NEG_INF = -1e30
DT_MIN = 1e-3
DT_MAX = 1e-1
FORGET_BIAS_LO = 2.0
FORGET_BIAS_HI = 8.0

kernel_name = "hymba_fox_s5_decode_step"


def _rmsnorm(x, g):
    xf = x.astype(jnp.float32)
    r = lax.rsqrt(jnp.mean(xf * xf, axis=-1, keepdims=True) + EPS)
    return (xf * r * g.astype(jnp.float32)).astype(x.dtype)


def _project(xn, w_in, b_f):
    z = xn @ w_in
    a = ATT_WIDTH
    q = z[..., :a]
    k = z[..., a:2 * a]
    v = z[..., 2 * a:3 * a]
    f_logit = z[..., 3 * a:3 * a + N_HEADS]
    u = z[..., 3 * a + N_HEADS:]
    shp = z.shape[:-1] + (N_HEADS, HEAD_DIM)
    logf = jax.nn.log_sigmoid(f_logit.astype(jnp.float32) + b_f.astype(jnp.float32))
    return q.reshape(shp), k.reshape(shp), v.reshape(shp), logf, u


def _fox_prompt(q, k, v, logf):
    b, seq = q.shape[0], q.shape[1]
    scale = HEAD_DIM ** -0.5
    c = lax.cumsum(logf, axis=1).transpose(0, 2, 1)
    k_pos = jnp.arange(seq)

    def one_block(i):
        start = i * BLOCK_Q
        qb = lax.dynamic_slice_in_dim(q, start, BLOCK_Q, axis=1)
        cq = lax.dynamic_slice_in_dim(c, start, BLOCK_Q, axis=2)
        logits = jnp.einsum("bqhd,bkhd->bhqk", qb, k).astype(jnp.float32) * scale
        logits = logits + cq[..., :, None] - c[..., None, :]
        q_pos = start + jnp.arange(BLOCK_Q)
        mask = k_pos[None, :] <= q_pos[:, None]
        p = jax.nn.softmax(jnp.where(mask, logits, NEG_INF), axis=-1).astype(v.dtype)
        return jnp.einsum("bhqk,bkhd->bqhd", p, v)

    out = lax.map(one_block, jnp.arange(seq // BLOCK_Q))
    return out.transpose(1, 0, 2, 3, 4).reshape(b, seq, ATT_WIDTH)


def _fox_sample(q, k_new, v_new, logf_new, k_past, v_past, logf_past):
    b, t = q.shape[0], q.shape[1]
    past = k_past.shape[1]
    scale = HEAD_DIM ** -0.5
    r_past = lax.cumsum(logf_past, axis=1, reverse=True) - logf_past
    c_new = lax.cumsum(logf_new, axis=1)
    c_key = jnp.concatenate([-r_past, c_new], axis=1).transpose(0, 2, 1)
    c_q = c_new.transpose(0, 2, 1)
    k_all = jnp.concatenate([k_past.astype(k_new.dtype), k_new], axis=1)
    v_all = jnp.concatenate([v_past.astype(v_new.dtype), v_new], axis=1)
    logits = jnp.einsum("bqhd,bkhd->bhqk", q, k_all).astype(jnp.float32) * scale
    logits = logits + c_q[..., :, None] - c_key[..., None, :]
    mask = jnp.arange(past + t)[None, :] <= (past + jnp.arange(t))[:, None]
    p = jax.nn.softmax(jnp.where(mask, logits, NEG_INF), axis=-1).astype(v_all.dtype)
    out = jnp.einsum("bhqk,bkhd->bqhd", p, v_all)
    return out.reshape(b, t, ATT_WIDTH)


def _s5_op(e1, e2):
    a1r, a1i, b1r, b1i = e1
    a2r, a2i, b2r, b2i = e2
    return (a1r * a2r - a1i * a2i,
            a1r * a2i + a1i * a2r,
            a2r * b1r - a2i * b1i + b2r,
            a2r * b1i + a2i * b1r + b2i)


def _s5(u, h0_re, h0_im, a_re, a_im, log_dt, b_re, b_im, c_re, c_im, d_skip, w_glu):
    f32 = jnp.float32
    b, seq = u.shape[0], u.shape[1]
    uf = u.astype(f32).reshape(b, seq, N_SSM_GROUPS, SSM_GROUP)
    ar, ai = a_re.astype(f32), a_im.astype(f32)
    dt = jnp.exp(log_dt.astype(f32))[:, None]
    mag = jnp.exp(ar * dt)
    lam_re = mag * jnp.cos(ai * dt)
    lam_im = mag * jnp.sin(ai * dt)
    den = ar * ar + ai * ai
    q_re = ((lam_re - 1.0) * ar + lam_im * ai) / den
    q_im = (lam_im * ar - (lam_re - 1.0) * ai) / den
    br, bi = b_re.astype(f32), b_im.astype(f32)
    bb_re = q_re[..., None] * br - q_im[..., None] * bi
    bb_im = q_re[..., None] * bi + q_im[..., None] * br
    bu_re = jnp.einsum("blgc,gpc->blgp", uf, bb_re)
    bu_im = jnp.einsum("blgc,gpc->blgp", uf, bb_im)
    a_b_re = jnp.broadcast_to(lam_re, bu_re.shape)
    a_b_im = jnp.broadcast_to(lam_im, bu_im.shape)
    acc_re, acc_im, h_re, h_im = lax.associative_scan(
        _s5_op, (a_b_re, a_b_im, bu_re, bu_im), axis=1)
    h0r = h0_re.astype(f32)[:, None]
    h0i = h0_im.astype(f32)[:, None]
    h_re, h_im = (h_re + acc_re * h0r - acc_im * h0i,
                  h_im + acc_re * h0i + acc_im * h0r)
    y = (jnp.einsum("gcp,blgp->blgc", c_re.astype(f32), h_re)
         - jnp.einsum("gcp,blgp->blgc", c_im.astype(f32), h_im)
         + d_skip.astype(f32).reshape(N_SSM_GROUPS, SSM_GROUP) * uf)
    y = jax.nn.gelu(y.reshape(b, seq, SSM_WIDTH)).astype(u.dtype)
    gv = y @ w_glu
    out = gv[..., :SSM_WIDTH] * jax.nn.sigmoid(gv[..., SSM_WIDTH:])
    return out, h_re[:, -1], h_im[:, -1]


def _layer(x, attend, h0_re, h0_im, w):
    xn = _rmsnorm(x, w["norm1_g"])
    q, k, v, logf, u = _project(xn, w["w_in"], w["b_f"])
    att = attend(q, k, v, logf)
    ssm_out, h_re, h_im = _s5(u, h0_re, h0_im, w["a_re"], w["a_im"], w["log_dt"],
                              w["b_re"], w["b_im"], w["c_re"], w["c_im"], w["d"], w["w_glu"])
    mixed = jnp.concatenate([_rmsnorm(att, w["norm_attn_g"]),
                             _rmsnorm(ssm_out, w["norm_ssm_g"])], axis=-1)
    h = x + mixed @ w["w_out"]
    hn = _rmsnorm(h, w["norm2_g"])
    y = h + (jax.nn.silu(hn @ w["w_gate"]) * (hn @ w["w_up"])) @ w["w_down"]
    return y, k, v, logf, h_re, h_im


def setup_inputs(seed: int = 0) -> dict:
    key = jax.random.key(seed)
    ks = jax.random.split(key, 32)
    f32 = jnp.float32
    n_pages = PAST_LEN // PAGE_SIZE
    n_pool = (DEC_BATCH * n_pages * 5) // 4
    G, P = N_SSM_GROUPS, STATE_DIM

    def nrm(k, shape, scale):
        return scale * jax.random.normal(k, shape, f32)

    head_bias = jnp.linspace(FORGET_BIAS_LO, FORGET_BIAS_HI, N_HEADS, dtype=f32)
    x_prompt = nrm(ks[0], (BATCH, SEQ, D_MODEL), 1.0)
    x_sample = nrm(ks[1], (DEC_BATCH, DEC_SEQ, D_MODEL), 1.0)
    cache_k = nrm(ks[2], (DEPTH, n_pool, PAGE_SIZE, N_HEADS, HEAD_DIM), 1.0)
    cache_v = nrm(ks[3], (DEPTH, n_pool, PAGE_SIZE, N_HEADS, HEAD_DIM), 1.0)
    cache_logf = jax.nn.log_sigmoid(head_bias + nrm(ks[4], (DEPTH, n_pool, PAGE_SIZE, N_HEADS), 1.0))
    state_ssm_re = nrm(ks[5], (DEPTH, DEC_BATCH, G, P), 0.5)
    state_ssm_im = nrm(ks[6], (DEPTH, DEC_BATCH, G, P), 0.5)
    page_table = jax.random.permutation(ks[7], n_pool)[: DEC_BATCH * n_pages].reshape(
        DEC_BATCH, n_pages).astype(jnp.int32)
    norm1_g = 1.0 + nrm(ks[8], (DEPTH, D_MODEL), 0.02)
    w_in = nrm(ks[9], (DEPTH, D_MODEL, IN_COLS), D_MODEL ** -0.5)
    b_f = head_bias + nrm(ks[10], (DEPTH, N_HEADS), 0.1)
    ssm_a_re = -0.5 + nrm(ks[11], (DEPTH, G, P), 0.01)
    ssm_a_im = math.pi * jnp.arange(P, dtype=f32) + nrm(ks[12], (DEPTH, G, P), 0.01)
    ssm_log_dt = jax.random.uniform(ks[13], (DEPTH, G), f32, math.log(DT_MIN), math.log(DT_MAX))
    ssm_b_re = nrm(ks[14], (DEPTH, G, P, SSM_GROUP), (2 * SSM_GROUP) ** -0.5)
    ssm_b_im = nrm(ks[15], (DEPTH, G, P, SSM_GROUP), (2 * SSM_GROUP) ** -0.5)
    ssm_c_re = nrm(ks[16], (DEPTH, G, SSM_GROUP, P), (2 * P) ** -0.5)
    ssm_c_im = nrm(ks[17], (DEPTH, G, SSM_GROUP, P), (2 * P) ** -0.5)
    ssm_d = nrm(ks[18], (DEPTH, SSM_WIDTH), 1.0)
    w_glu = nrm(ks[19], (DEPTH, SSM_WIDTH, 2 * SSM_WIDTH), SSM_WIDTH ** -0.5)
    norm_attn_g = 1.0 + nrm(ks[20], (DEPTH, ATT_WIDTH), 0.02)
    norm_ssm_g = 1.0 + nrm(ks[21], (DEPTH, SSM_WIDTH), 0.02)
    w_out = nrm(ks[22], (DEPTH, D_MODEL, D_MODEL), D_MODEL ** -0.5)
    norm2_g = 1.0 + nrm(ks[23], (DEPTH, D_MODEL), 0.02)
    w_gate = nrm(ks[24], (DEPTH, D_MODEL, D_FF), D_MODEL ** -0.5)
    w_up = nrm(ks[25], (DEPTH, D_MODEL, D_FF), D_MODEL ** -0.5)
    w_down = nrm(ks[26], (DEPTH, D_FF, D_MODEL), D_FF ** -0.5)
    norm_f_g = 1.0 + nrm(ks[27], (D_MODEL,), 0.02)
    return {
        "x_prompt": x_prompt, "x_sample": x_sample,
        "cache_k": cache_k, "cache_v": cache_v, "cache_logf": cache_logf,
        "state_ssm_re": state_ssm_re, "state_ssm_im": state_ssm_im,
        "page_table": page_table,
        "norm1_g": norm1_g, "w_in": w_in, "b_f": b_f,
        "ssm_a_re": ssm_a_re, "ssm_a_im": ssm_a_im, "ssm_log_dt": ssm_log_dt,
        "ssm_b_re": ssm_b_re, "ssm_b_im": ssm_b_im, "ssm_c_re": ssm_c_re, "ssm_c_im": ssm_c_im,
        "ssm_d": ssm_d, "w_glu": w_glu,
        "norm_attn_g": norm_attn_g, "norm_ssm_g": norm_ssm_g, "w_out": w_out,
        "norm2_g": norm2_g, "w_gate": w_gate, "w_up": w_up, "w_down": w_down,
        "norm_f_g": norm_f_g,
    }


def reference(x_prompt, x_sample, cache_k, cache_v, cache_logf, state_ssm_re, state_ssm_im,
              page_table, norm1_g, w_in, b_f, ssm_a_re, ssm_a_im, ssm_log_dt,
              ssm_b_re, ssm_b_im, ssm_c_re, ssm_c_im, ssm_d, w_glu,
              norm_attn_g, norm_ssm_g, w_out, norm2_g, w_gate, w_up, w_down, norm_f_g):
    n_prompt = x_prompt.shape[0]
    n_dec = x_sample.shape[0]
    yp, ys = x_prompt, x_sample
    kp_l, vp_l, lp_l, hrp_l, hip_l = [], [], [], [], []
    ks_l, vs_l, ls_l, hrs_l, his_l = [], [], [], [], []
    for l in range(DEPTH):
        w = {
            "norm1_g": norm1_g[l], "w_in": w_in[l], "b_f": b_f[l],
            "a_re": ssm_a_re[l], "a_im": ssm_a_im[l], "log_dt": ssm_log_dt[l],
            "b_re": ssm_b_re[l], "b_im": ssm_b_im[l], "c_re": ssm_c_re[l], "c_im": ssm_c_im[l],
            "d": ssm_d[l], "w_glu": w_glu[l],
            "norm_attn_g": norm_attn_g[l], "norm_ssm_g": norm_ssm_g[l], "w_out": w_out[l],
            "norm2_g": norm2_g[l], "w_gate": w_gate[l], "w_up": w_up[l], "w_down": w_down[l],
        }
        h0 = jnp.zeros((n_prompt, N_SSM_GROUPS, STATE_DIM), jnp.float32)
        yp, k, v, logf, hr, hi = _layer(yp, _fox_prompt, h0, h0, w)
        kp_l.append(k); vp_l.append(v); lp_l.append(logf); hrp_l.append(hr); hip_l.append(hi)
        k_past = cache_k[l][page_table].reshape(n_dec, -1, N_HEADS, HEAD_DIM)
        v_past = cache_v[l][page_table].reshape(n_dec, -1, N_HEADS, HEAD_DIM)
        lf_past = cache_logf[l][page_table].reshape(n_dec, -1, N_HEADS).astype(jnp.float32)
        attend_s = functools.partial(_fox_sample, k_past=k_past, v_past=v_past, logf_past=lf_past)
        ys, k, v, logf, hr, hi = _layer(ys, attend_s, state_ssm_re[l], state_ssm_im[l], w)
        ks_l.append(k); vs_l.append(v); ls_l.append(logf); hrs_l.append(hr); his_l.append(hi)
    y_prompt = _rmsnorm(yp, norm_f_g)
    y_sample = _rmsnorm(ys, norm_f_g)
    new_k_prompt = jnp.stack(kp_l)
    new_v_prompt = jnp.stack(vp_l)
    new_logf_prompt = jnp.stack(lp_l)
    new_ssm_re_prompt = jnp.stack(hrp_l)
    new_ssm_im_prompt = jnp.stack(hip_l)
    new_k_sample = jnp.stack(ks_l)
    new_v_sample = jnp.stack(vs_l)
    new_logf_sample = jnp.stack(ls_l)
    new_ssm_re_sample = jnp.stack(hrs_l)
    new_ssm_im_sample = jnp.stack(his_l)
    return (y_prompt, y_sample, new_k_prompt, new_v_prompt, new_logf_prompt,
            new_ssm_re_prompt, new_ssm_im_prompt, new_k_sample, new_v_sample, new_logf_sample,
            new_ssm_re_sample, new_ssm_im_sample)
```

```python
import functools
import math

import jax
import jax.numpy as jnp
import numpy as np
from jax import lax
from jax.experimental import pallas as pl
from jax.experimental.pallas import tpu as pltpu

F32 = jnp.float32
BF16 = jnp.bfloat16

HEAD_DIM = 64
SSM_GROUP = 16
STATE_DIM = 64
EPS = 1e-6
NEG = -1e30
LANES = 128
CHUNK = 8
VMEM_LIMIT = 56 * 1024 * 1024


def _dot(a, b):
    return jnp.dot(a, b, preferred_element_type=F32)


def _dot_nt(a, b):
    return lax.dot_general(a, b, (((1,), (1,)), ((), ())), preferred_element_type=F32)


def _dot_tn(a, b):
    return lax.dot_general(a, b, (((0,), (0,)), ((), ())), preferred_element_type=F32)


def _split3(x):
    hi = x.astype(BF16)
    r1 = x - hi.astype(F32)
    mid = r1.astype(BF16)
    lo = (r1 - mid.astype(F32)).astype(BF16)
    return hi, mid, lo


def _sum3(x, n, axis):
    if axis == 0:
        return x[:n] + x[n:2 * n] + x[2 * n:3 * n]
    return x[:, :n] + x[:, n:2 * n] + x[:, 2 * n:3 * n]


def _rms_rows(x, g):
    r = lax.rsqrt(jnp.mean(x * x, axis=-1, keepdims=True) + EPS)
    return x * r * g


def _log_sigmoid(x):
    return jnp.minimum(x, 0.0) - jnp.log1p(jnp.exp(-jnp.abs(x)))


def _params(sem, vmem=VMEM_LIMIT):
    return pltpu.CompilerParams(dimension_semantics=sem, vmem_limit_bytes=vmem)


def _const_spec(shape):
    nd = len(shape)
    return pl.BlockSpec(shape, lambda *_: (0,) * nd, pipeline_mode=pl.Buffered(1))


def _proj_kernel(x_ref, g_ref, wn_ref, wf_ref, bf_ref, wt_ref, wft_ref, bft_ref, pk_ref, pq_ref,
                 *out_and_scratch, prompt, tm, att, heads, tk):
    if prompt:
        (u_ref, kt_ref, vt32_ref, lft_ref, ka_ref, qta_ref, vt_ref, car_ref, cart_ref) = out_and_scratch
    else:
        (u_ref, q_ref, k_ref, v_ref, lf_ref, cnt_ref) = out_and_scratch
    i = pl.program_id(1)
    xn = _rms_rows(x_ref[...], g_ref[...]).astype(BF16)
    z = _dot(xn, wn_ref[...])
    if prompt:
        k, u = z[:, :att], z[:, att:]
    else:
        q, k, v, u = z[:, :att], z[:, att:2 * att], z[:, 2 * att:3 * att], z[:, 3 * att:]
        q_ref[...] = q
        k_ref[...] = k
        v_ref[...] = v
    for s in range(u.shape[1] // LANES):
        u_ref[s] = u[:, s * LANES:(s + 1) * LANES]
    lft = _log_sigmoid(_dot_nt(wft_ref[...], xn) + bft_ref[...])
    row = lax.broadcasted_iota(jnp.int32, (tm, tm), 0)
    col = lax.broadcasted_iota(jnp.int32, (tm, tm), 1)
    if prompt:
        triu = jnp.where(row <= col, 1.0, 0.0).astype(BF16)
    else:
        same = (row // CHUNK) == (col // CHUNK)
        triu = jnp.where(same & (row <= col), 1.0, 0.0).astype(BF16)
    hi, mid, lo = _split3(lft[:heads])
    zpad = jnp.zeros((heads, tm), BF16)
    ct = _sum3(_dot(jnp.concatenate([hi, mid, lo, zpad], axis=0), triu), heads, 0)
    if not prompt:
        lf_ref[...] = _log_sigmoid(_dot(xn, wf_ref[...]) + bf_ref[...])[:, :heads]
        cnt_ref[...] = ct
        return

    @pl.when(i == 0)
    def _():
        car_ref[...] = jnp.zeros_like(car_ref)
        cart_ref[...] = jnp.zeros_like(cart_ref)

    lft_ref[0] = lft[:heads]
    ct = ct + cart_ref[:, 0:1]
    cart_ref[...] = jnp.broadcast_to(ct[:, tm - 1:tm], cart_ref.shape)
    lf = _log_sigmoid(_dot(xn, wf_ref[...]) + bf_ref[...])
    tril = jnp.where(col <= row, 1.0, 0.0).astype(BF16)
    hi, mid, lo = _split3(lf)
    c = _sum3(_dot(tril, jnp.concatenate([hi, mid, lo], axis=1)), LANES, 1) + car_ref[...]
    car_ref[...] = c[tm - 1:tm, :]
    lane = lax.broadcasted_iota(jnp.int32, (tm, LANES), 1)
    chi, cmid, clo = _split3(c)
    cpack = jnp.where(lane < heads, chi,
                      jnp.where(lane < 2 * heads, cmid,
                                jnp.where(lane < 3 * heads, clo,
                                          jnp.where(lane == 3 * heads, 1.0, 0.0).astype(BF16))))
    kaug = _dot(cpack, pk_ref[...])
    zt = _dot_nt(wt_ref[...], xn)
    srow = lax.broadcasted_iota(jnp.int32, (4 * heads, tm), 0)
    thi, tmid, tlo = _split3(ct)
    ctpack = jnp.concatenate([thi, tmid, tlo, jnp.ones((heads, tm), BF16)], axis=0)
    ctpack = jnp.where(srow <= 3 * heads, ctpack, jnp.zeros_like(ctpack))
    qaug = _dot(pq_ref[...], ctpack)
    for h in range(heads):
        sl = slice(h * HEAD_DIM, (h + 1) * HEAD_DIM)
        kt = zt[att + h * HEAD_DIM:att + (h + 1) * HEAD_DIM, :]
        vt = zt[2 * att + h * HEAD_DIM:2 * att + (h + 1) * HEAD_DIM, :]
        kt_ref[0, h] = kt
        vt32_ref[0, h] = vt
        ka_ref[0, h, :, 0:HEAD_DIM] = k[:, sl].astype(BF16)
        ka_ref[0, h, :, HEAD_DIM:2 * HEAD_DIM] = kaug[:, sl].astype(BF16)
        qta_ref[0, h, 0:HEAD_DIM, :] = zt[sl, :].astype(BF16)
        qta_ref[0, h, HEAD_DIM:2 * HEAD_DIM, :] = qaug[sl, :].astype(BF16)
        for jj in range(tm // tk):
            vt_ref[0, h, jj] = vt[:, jj * tk:(jj + 1) * tk].astype(BF16)


def _proj(x, g1, wn, wf, bf, wt, wft, bft, pk, pq, *, prompt, tm, tk):
    b, l, d = x.shape
    att = d // 2
    heads = att // HEAD_DIM
    nslab = (d - att) // LANES
    t = b * l
    nb = l // tm
    x2 = x.reshape(t, d)
    tok = lambda bi, i: (bi * nb + i, 0)
    out_shape = [jax.ShapeDtypeStruct((nslab, t, LANES), F32)]
    out_specs = [pl.BlockSpec((nslab, tm, LANES), lambda bi, i: (0, bi * nb + i, 0))]
    scratch = []
    if prompt:
        hd_t = pl.BlockSpec((1, heads, HEAD_DIM, tm), lambda bi, i: (bi, 0, 0, i))
        out_shape += [jax.ShapeDtypeStruct((b, heads, HEAD_DIM, l), F32),
                      jax.ShapeDtypeStruct((b, heads, HEAD_DIM, l), F32),
                      jax.ShapeDtypeStruct((b, heads, l), F32),
                      jax.ShapeDtypeStruct((b, heads, l, 2 * HEAD_DIM), BF16),
                      jax.ShapeDtypeStruct((b, heads, 2 * HEAD_DIM, l), BF16),
                      jax.ShapeDtypeStruct((b, heads, l // tk, HEAD_DIM, tk), BF16)]
        out_specs += [hd_t, hd_t,
                      pl.BlockSpec((1, heads, tm), lambda bi, i: (bi, 0, i)),
                      pl.BlockSpec((1, heads, tm, 2 * HEAD_DIM), lambda bi, i: (bi, 0, i, 0)),
                      pl.BlockSpec((1, heads, 2 * HEAD_DIM, tm), lambda bi, i: (bi, 0, 0, i)),
                      pl.BlockSpec((1, heads, tm // tk, HEAD_DIM, tk), lambda bi, i: (bi, 0, i, 0, 0))]
        scratch = [pltpu.VMEM((1, LANES), F32), pltpu.VMEM((heads, LANES), F32)]
    else:
        out_shape += [jax.ShapeDtypeStruct((t, att), F32)] * 3 + [
            jax.ShapeDtypeStruct((t, heads), F32), jax.ShapeDtypeStruct((heads, t), F32)]
        out_specs += [pl.BlockSpec((tm, att), tok)] * 3 + [
            pl.BlockSpec((tm, heads), tok), pl.BlockSpec((heads, tm), lambda bi, i: (0, bi * nb + i))]
    consts = (g1, wn, wf, bf, wt, wft, bft, pk, pq)
    return pl.pallas_call(
        functools.partial(_proj_kernel, prompt=prompt, tm=tm, att=att, heads=heads, tk=tk),
        grid=(b, nb),
        in_specs=[pl.BlockSpec((tm, d), tok)] + [_const_spec(c.shape) for c in consts],
        out_specs=out_specs, out_shape=out_shape, scratch_shapes=scratch,
        compiler_params=_params(("arbitrary", "arbitrary")),
        name="proj_prompt" if prompt else "proj_sample",
    )(x2, *consts)


def _fox_kernel(ka_ref, qta_ref, vt_ref, o_ref, *, tq, tk):
    qi = pl.program_id(2)
    qt = qta_ref[0, 0]
    ratio = tq // tk

    def block(j, carry, masked):
        m, l, acc = carry
        s = _dot(ka_ref[0, 0, j], qt)
        if masked is not None:
            s = jnp.where(masked, s, NEG)
        m_new = jnp.maximum(m, jnp.max(s, axis=0, keepdims=True))
        alpha = jnp.exp(m - m_new)
        p = jnp.exp(s - m_new)
        l = alpha * l + jnp.sum(p, axis=0, keepdims=True)
        acc = alpha * acc + _dot(vt_ref[0, 0, j], p.astype(BF16))
        return m_new, l, acc

    init = (jnp.full((1, tq), NEG, F32), jnp.zeros((1, tq), F32), jnp.zeros((HEAD_DIM, tq), F32))
    carry = lax.fori_loop(0, qi * ratio, lambda j, c: block(j, c, None), init)
    key = lax.broadcasted_iota(jnp.int32, (tk, tq), 0)
    qry = lax.broadcasted_iota(jnp.int32, (tk, tq), 1)
    for d in range(ratio):
        carry = block(qi * ratio + d, carry, key + d * tk <= qry)
    m, l, acc = carry
    o_ref[0] = acc / l


def _fox(ka, qta, vt, *, tq, tk):
    b, heads, l, _ = ka.shape
    ka5 = ka.reshape(b, heads, l // tk, tk, 2 * HEAD_DIM)
    return pl.pallas_call(
        functools.partial(_fox_kernel, tq=tq, tk=tk),
        grid=(b, heads, l // tq),
        in_specs=[pl.BlockSpec((1, 1, l // tk, tk, 2 * HEAD_DIM), lambda bi, h, i: (bi, h, 0, 0, 0)),
                  pl.BlockSpec((1, 1, 2 * HEAD_DIM, tq), lambda bi, h, i: (bi, h, 0, i)),
                  pl.BlockSpec((1, 1, l // tk, HEAD_DIM, tk), lambda bi, h, i: (bi, h, 0, 0, 0))],
        out_specs=pl.BlockSpec((1, HEAD_DIM, tq), lambda bi, h, i: (bi, h, i)),
        out_shape=jax.ShapeDtypeStruct((b, heads * HEAD_DIM, l), F32),
        compiler_params=_params(("arbitrary", "arbitrary", "arbitrary")),
        name="fox_prompt",
    )(ka5, qta, vt)


def _pastbias_kernel(pt_ref, lf_hbm, umat_ref, later_ref, o_ref, xbuf, sems, *, npages, heads):
    b = pl.program_id(0)
    nb = pl.num_programs(0)
    slot = lax.rem(b, 2)
    rows = npages * heads

    def copy(bb, sl, p):
        return pltpu.make_async_copy(lf_hbm.at[pt_ref[bb, p]], xbuf.at[sl, pl.ds(p * heads, heads)],
                                     sems.at[sl, p])

    def start_all(bb, sl):
        for p in range(npages):
            copy(bb, sl, p).start()

    @pl.when(b == 0)
    def _():
        start_all(0, 0)

    @pl.when(b + 1 < nb)
    def _():
        start_all(b + 1, 1 - slot)

    for p in range(npages):
        copy(b, slot, p).wait()

    x = xbuf[slot]
    hi, mid, lo = _split3(x)
    both = _sum3(_dot(jnp.concatenate([hi, mid, lo], axis=0), umat_ref[...]), rows, 0)
    rin = both[:, :LANES]
    tot = both[:, LANES:]
    hi, mid, lo = _split3(tot)
    off = _sum3(_dot(later_ref[...], jnp.concatenate([hi, mid, lo], axis=1)), LANES, 1)
    o_ref[0] = rin + off


def _pastbias(page_table, lf_t):
    n_pool, heads, page = lf_t.shape
    db, npages = page_table.shape
    assert page == LANES
    rows = npages * heads
    s = np.arange(page)
    umat = np.concatenate([s[:, None] > s[None, :], np.ones((page, page), bool)], axis=1)
    r = np.arange(rows)
    later = (r[None, :] // heads > r[:, None] // heads) & (r[None, :] % heads == r[:, None] % heads)
    umat, later = jnp.asarray(umat, BF16), jnp.asarray(later, BF16)
    grid_spec = pltpu.PrefetchScalarGridSpec(
        num_scalar_prefetch=1, grid=(db,),
        in_specs=[pl.BlockSpec(memory_space=pl.ANY),
                  pl.BlockSpec(umat.shape, lambda b, pt: (0, 0)),
                  pl.BlockSpec(later.shape, lambda b, pt: (0, 0))],
        out_specs=pl.BlockSpec((1, rows, page), lambda b, pt: (b, 0, 0)),
        scratch_shapes=[pltpu.VMEM((2, rows, page), F32), pltpu.SemaphoreType.DMA((2, npages))])
    return pl.pallas_call(
        functools.partial(_pastbias_kernel, npages=npages, heads=heads),
        grid_spec=grid_spec,
        out_shape=jax.ShapeDtypeStruct((db, rows, page), F32),
        compiler_params=_params(("arbitrary",)),
        name="pastbias",
    )(page_table, lf_t, umat, later)


def _decode_kernel(pt_ref, q_ref, kn_ref, vn_ref, cnt_ref, bias_ref, *rest, pps, heads, tnew):
    k_refs = rest[:pps]
    v_refs = rest[pps:2 * pps]
    o_ref, qbd_ref, m_ref, l_ref, acc_ref = rest[2 * pps:]
    s = pl.program_id(1)
    ns = pl.num_programs(1)
    rows = heads * tnew
    width = heads * HEAD_DIM
    page = k_refs[0].shape[-1]
    rhead = lax.broadcasted_iota(jnp.int32, (rows, width), 0) // tnew
    chead = lax.broadcasted_iota(jnp.int32, (rows, width), 1) // HEAD_DIM

    @pl.when(s == 0)
    def _():
        qtile = jnp.concatenate([q_ref[...]] * heads, axis=0)
        qbd_ref[...] = jnp.where(rhead == chead, qtile, 0.0).astype(BF16)
        m_ref[...] = jnp.full_like(m_ref, NEG)
        l_ref[...] = jnp.zeros_like(l_ref)
        acc_ref[...] = jnp.zeros_like(acc_ref)

    qbd = qbd_ref[...]

    def cat_pages(refs, g):
        return jnp.concatenate([refs[2 * g + e][0].reshape(width, page).astype(BF16) for e in range(2)], axis=1)

    scores = []
    for g in range(pps // 2):
        bias = jnp.concatenate(
            [jnp.concatenate([jnp.broadcast_to(bias_ref[0, pl.ds((2 * g + e) * heads + h, 1), :], (tnew, page))
                              for h in range(heads)], axis=0) for e in range(2)], axis=1)
        scores.append(_dot(qbd, cat_pages(k_refs, g)) + bias)
    m_old = m_ref[...]
    m_new = m_old
    for st in scores:
        m_new = jnp.maximum(m_new, jnp.max(st, axis=1, keepdims=True))
    alpha = jnp.exp(m_old - m_new)
    l = alpha * l_ref[...]
    acc = alpha * acc_ref[...]
    for g in range(pps // 2):
        p = jnp.exp(scores[g] - m_new)
        l = l + jnp.sum(p, axis=1, keepdims=True)
        acc = acc + _dot_nt(p.astype(BF16), cat_pages(v_refs, g))
    m_ref[...] = m_new
    l_ref[...] = l
    acc_ref[...] = acc

    @pl.when(s == ns - 1)
    def _():
        st = _dot_nt(qbd, kn_ref[...].astype(BF16))
        bias = jnp.concatenate(
            [jnp.broadcast_to(cnt_ref[0, h:h + 1, :], (tnew, tnew)) for h in range(heads)], axis=0)
        qi = lax.broadcasted_iota(jnp.int32, (rows, tnew), 0) % tnew
        kj = lax.broadcasted_iota(jnp.int32, (rows, tnew), 1)
        st = jnp.where(kj <= qi, st - bias, NEG)
        m_fin = jnp.maximum(m_new, jnp.max(st, axis=1, keepdims=True))
        a2 = jnp.exp(m_new - m_fin)
        p = jnp.exp(st - m_fin)
        lf = a2 * l + jnp.sum(p, axis=1, keepdims=True)
        accf = a2 * acc + _dot(p.astype(BF16), vn_ref[...].astype(BF16))
        res = jnp.where(rhead == chead, accf / lf, 0.0)
        out = res[0:tnew]
        for h in range(1, heads):
            out = out + res[h * tnew:(h + 1) * tnew]
        o_ref[...] = out


def _decode(page_table, q, kn, vn, cnt, bias, ck_t, cv_t, *, tnew, pps):
    n_pool, heads, hd, page = ck_t.shape
    db, npages = page_table.shape
    width = heads * hd
    rows = heads * tnew
    assert pps % 2 == 0 and npages % pps == 0
    tokspec = pl.BlockSpec((tnew, width), lambda b, s, pt: (b, 0))

    def page_spec(i):
        return pl.BlockSpec((1, heads, hd, page), lambda b, s, pt: (pt[b, s * pps + i], 0, 0, 0))

    grid_spec = pltpu.PrefetchScalarGridSpec(
        num_scalar_prefetch=1, grid=(db, npages // pps),
        in_specs=[tokspec, tokspec, tokspec,
                  pl.BlockSpec((1, heads, tnew), lambda b, s, pt: (b, 0, 0)),
                  pl.BlockSpec((1, pps * heads, page), lambda b, s, pt: (b, s, 0))]
                 + [page_spec(i) for i in range(pps)] * 2,
        out_specs=tokspec,
        scratch_shapes=[pltpu.VMEM((rows, width), BF16), pltpu.VMEM((rows, 1), F32),
                        pltpu.VMEM((rows, 1), F32), pltpu.VMEM((rows, width), F32)])
    return pl.pallas_call(
        functools.partial(_decode_kernel, pps=pps, heads=heads, tnew=tnew),
        grid_spec=grid_spec,
        out_shape=jax.ShapeDtypeStruct((db * tnew, width), F32),
        compiler_params=_params(("arbitrary", "arbitrary")),
        name="decode_sample",
    )(page_table, q, kn, vn, cnt, bias, *([ck_t] * pps), *([cv_t] * pps))


def _s5_kernel(u_ref, b_ref, c_ref, lpr_ref, lpi_ref, d_ref, *rest, has_init, n, half):
    if has_init:
        h0r_ref, h0i_ref, y_ref, hr_ref, hi_ref, hlr, hli = rest
    else:
        y_ref, hr_ref, hi_ref, hlr, hli, hpr, hpi, car, cai = rest
    nslab = u_ref.shape[0]
    hs = nslab // 2

    def u_rows(j):
        return [u_ref[s, pl.ds(j, n, stride=CHUNK), :] for s in range(nslab)]

    lr1, li1 = lpr_ref[0:1, :], lpi_ref[0:1, :]
    hr = hi = None
    for j in range(CHUNK):
        us = u_rows(j)
        ua = jnp.concatenate(us[:hs], axis=1).astype(BF16)
        ub = jnp.concatenate(us[hs:], axis=1).astype(BF16)
        ba = _dot(ua, b_ref[0])
        bb = _dot(ub, b_ref[1])
        bur = jnp.concatenate([ba[:, :half], bb[:, :half]], axis=1)
        bui = jnp.concatenate([ba[:, half:], bb[:, half:]], axis=1)
        if hr is None:
            hr, hi = bur, bui
        else:
            hr, hi = lr1 * hr - li1 * hi + bur, lr1 * hi + li1 * hr + bui
        hlr[j] = hr
        hli[j] = hi

    if has_init:
        pr_all, pi_all = h0r_ref[...], h0i_ref[...]
    else:
        i = pl.program_id(1)

        @pl.when(i == 0)
        def _():
            car[...] = jnp.zeros_like(car)
            cai[...] = jnp.zeros_like(cai)

        ltr, lti = lpr_ref[CHUNK - 1:CHUNK, :], lpi_ref[CHUNK - 1:CHUNK, :]

        def step(c, carry):
            pr, pi_ = carry
            hpr[pl.ds(c, 1), :] = pr
            hpi[pl.ds(c, 1), :] = pi_
            er = hlr[CHUNK - 1, pl.ds(c, 1), :]
            ei = hli[CHUNK - 1, pl.ds(c, 1), :]
            return ltr * pr - lti * pi_ + er, ltr * pi_ + lti * pr + ei

        pr, pi_ = lax.fori_loop(0, n, step, (car[...], cai[...]))
        car[...] = pr
        cai[...] = pi_
        hr_ref[0] = pr
        hi_ref[0] = pi_
        pr_all, pi_all = hpr[...], hpi[...]

    for j in range(CHUNK):
        lr, li = lpr_ref[j:j + 1, :], lpi_ref[j:j + 1, :]
        fr = hlr[j] + lr * pr_all - li * pi_all
        fi = hli[j] + lr * pi_all + li * pr_all
        if has_init and j == CHUNK - 1:
            hr_ref[...] = fr
            hi_ref[...] = fi
        la = jnp.concatenate([fr[:, :half], fi[:, :half]], axis=1).astype(BF16)
        lb = jnp.concatenate([fr[:, half:], fi[:, half:]], axis=1).astype(BF16)
        y = jnp.concatenate([_dot(la, c_ref[0]), _dot(lb, c_ref[1])], axis=1)
        us = u_rows(j)
        for s in range(nslab):
            y_ref[s, pl.ds(j, n, stride=CHUNK), :] = (
                y[:, s * LANES:(s + 1) * LANES] + d_ref[:, s * LANES:(s + 1) * LANES] * us[s])


def _s5(u_slabs, bmat, cmat, lpr, lpi, dskip, h0=None, *, nseq, tm):
    nslab, t, _ = u_slabs.shape
    l = t // nseq
    n = tm // CHUNK
    sw = lpr.shape[1]
    half = sw // 2
    has_init = h0 is not None
    consts = (bmat, cmat, lpr, lpi, dskip)
    scratch = [pltpu.VMEM((CHUNK, n, sw), F32), pltpu.VMEM((CHUNK, n, sw), F32)]
    if has_init:
        nb = t // tm
        grid = (1, nb)
        slab_spec = pl.BlockSpec((nslab, tm, LANES), lambda bi, i: (0, i, 0))
        st_spec = pl.BlockSpec((n, sw), lambda bi, i: (i, 0))
        ins = [u_slabs, *consts, h0[0], h0[1]]
        in_specs = [slab_spec] + [_const_spec(c.shape) for c in consts] + [st_spec, st_spec]
        st_shape = jax.ShapeDtypeStruct((t // CHUNK, sw), F32)
    else:
        nb = l // tm
        grid = (nseq, nb)
        slab_spec = pl.BlockSpec((nslab, tm, LANES), lambda bi, i: (0, bi * nb + i, 0))
        st_spec = pl.BlockSpec((1, 1, sw), lambda bi, i: (bi, 0, 0))
        ins = [u_slabs, *consts]
        in_specs = [slab_spec] + [_const_spec(c.shape) for c in consts]
        st_shape = jax.ShapeDtypeStruct((nseq, 1, sw), F32)
        scratch += [pltpu.VMEM((n, sw), F32), pltpu.VMEM((n, sw), F32),
                    pltpu.VMEM((1, sw), F32), pltpu.VMEM((1, sw), F32)]
    return pl.pallas_call(
        functools.partial(_s5_kernel, has_init=has_init, n=n, half=half),
        grid=grid, in_specs=in_specs,
        out_specs=[slab_spec, st_spec, st_spec],
        out_shape=[jax.ShapeDtypeStruct((nslab, t, LANES), F32), st_shape, st_shape],
        scratch_shapes=scratch,
        compiler_params=_params(("arbitrary", "arbitrary")),
        name="s5_sample" if has_init else "s5_prompt",
    )(*ins)


def _gelu_tanh(x):
    return x * (0.5 * (1.0 + jnp.tanh(math.sqrt(2.0 / math.pi) * (x + 0.044715 * (x * x * x)))))


def _tail_kernel(x_ref, att_ref, y_ref, wglu_ref, gatt_ref, gssm_ref, woa_ref, wos_ref, g2_ref,
                 wg_ref, wu_ref, wd_ref, gf_ref, o_ref, *, att_transposed, sw):
    x = x_ref[...]
    yv = jnp.concatenate([y_ref[s] for s in range(y_ref.shape[0])], axis=1)
    gv = _dot(_gelu_tanh(yv).astype(BF16), wglu_ref[...])
    ssm = gv[:, :sw] * jax.nn.sigmoid(gv[:, sw:])
    ssm_n = _rms_rows(ssm, gssm_ref[...]).astype(BF16)
    if att_transposed:
        at = att_ref[0]
        r = lax.rsqrt(jnp.mean(at * at, axis=0, keepdims=True) + EPS)
        att_n = (at * r * gatt_ref[...]).astype(BF16)
        ho = _dot_tn(att_n, woa_ref[...])
    else:
        ho = _dot(_rms_rows(att_ref[...], gatt_ref[...]).astype(BF16), woa_ref[...])
    h = x + ho + _dot(ssm_n, wos_ref[...])
    hn = _rms_rows(h, g2_ref[...]).astype(BF16)
    gt = _dot(hn, wg_ref[...])
    act = (gt * jax.nn.sigmoid(gt) * _dot(hn, wu_ref[...])).astype(BF16)
    y = h + _dot(act, wd_ref[...])
    o_ref[...] = _rms_rows(y, gf_ref[...])


def _tail(x, att, y_slabs, wglu, gatt, gssm, woa, wos, g2, wg, wu, wd, gf, *, att_transposed, tm):
    b, l, d = x.shape
    t = b * l
    nb = l // tm
    nslab = y_slabs.shape[0]
    sw = nslab * LANES
    tok = lambda bi, i: (bi * nb + i, 0)
    if att_transposed:
        att_spec = pl.BlockSpec((1, att.shape[1], tm), lambda bi, i: (bi, 0, i))
    else:
        att_spec = pl.BlockSpec((tm, att.shape[1]), tok)
    consts_a = (wglu, gatt, gssm, woa, wos, g2, wg, wu, wd, gf)
    out = pl.pallas_call(
        functools.partial(_tail_kernel, att_transposed=att_transposed, sw=sw),
        grid=(b, nb),
        in_specs=[pl.BlockSpec((tm, d), tok), att_spec,
                  pl.BlockSpec((nslab, tm, LANES), lambda bi, i: (0, bi * nb + i, 0))]
                 + [_const_spec(c.shape) for c in consts_a],
        out_specs=pl.BlockSpec((tm, d), tok),
        out_shape=jax.ShapeDtypeStruct((t, d), F32),
        compiler_params=_params(("arbitrary", "arbitrary")),
        name="tail_prompt" if att_transposed else "tail_sample",
    )(x.reshape(t, d), att, y_slabs, *consts_a)
    return out.reshape(b, l, d)


def _blockdiag(m):
    g, r, c = m.shape
    eye = jnp.eye(g, dtype=m.dtype)
    return (m[:, :, None, :] * eye[:, None, :, None]).reshape(g * r, g * c)


def _s5_params(a_re, a_im, log_dt, b_re, b_im, c_re, c_im, d):
    ar, ai = a_re.astype(F32), a_im.astype(F32)
    g, p = ar.shape
    dt = jnp.exp(log_dt.astype(F32))[:, None]
    mag = jnp.exp(ar * dt)
    lam_re = mag * jnp.cos(ai * dt)
    lam_im = mag * jnp.sin(ai * dt)
    den = ar * ar + ai * ai
    q_re = ((lam_re - 1.0) * ar + lam_im * ai) / den
    q_im = (lam_im * ar - (lam_re - 1.0) * ai) / den
    br, bi = b_re.astype(F32), b_im.astype(F32)
    bb_re = (q_re[..., None] * br - q_im[..., None] * bi).transpose(0, 2, 1)
    bb_im = (q_re[..., None] * bi + q_im[..., None] * br).transpose(0, 2, 1)
    gh = g // 2
    bmat = jnp.stack([jnp.concatenate([_blockdiag(bb_re[s]), _blockdiag(bb_im[s])], axis=1)
                      for s in (slice(0, gh), slice(gh, g))]).astype(BF16)
    cr = c_re.astype(F32).transpose(0, 2, 1)
    ci = c_im.astype(F32).transpose(0, 2, 1)
    cmat = jnp.stack([jnp.concatenate([_blockdiag(cr[s]), -_blockdiag(ci[s])], axis=0)
                      for s in (slice(0, gh), slice(gh, g))]).astype(BF16)
    pr, pi_ = [lam_re], [lam_im]
    for _ in range(CHUNK - 1):
        pr, pi_ = (pr + [pr[-1] * lam_re - pi_[-1] * lam_im], pi_ + [pr[-1] * lam_im + pi_[-1] * lam_re])
    lpr = jnp.stack(pr).reshape(CHUNK, g * p)
    lpi = jnp.stack(pi_).reshape(CHUNK, g * p)
    return bmat, cmat, lpr, lpi, d.astype(F32).reshape(1, -1)


def _proj_params(w_in, b_f, norm1_g, d):
    att = d // 2
    heads = att // HEAD_DIM
    scale = HEAD_DIM ** -0.5
    w = w_in.astype(F32)
    wq = w[:, :att] * scale
    wk, wv = w[:, att:2 * att], w[:, 2 * att:3 * att]
    wfl = w[:, 3 * att:3 * att + heads]
    wu = w[:, 3 * att + heads:]
    wf = jnp.zeros((d, LANES), F32).at[:, :3 * heads].set(jnp.tile(wfl, (1, 3))).astype(BF16)
    bf = jnp.zeros((1, LANES), F32).at[0, :3 * heads].set(jnp.tile(b_f.astype(F32), 3))
    wft = jnp.zeros((2 * heads, d), F32).at[:heads].set(wfl.T).astype(BF16)
    bft = jnp.zeros((2 * heads, 1), F32).at[:heads, 0].set(b_f.astype(F32))
    pk = np.zeros((LANES, att), np.float32)
    pq = np.zeros((att, 4 * heads), np.float32)
    for h in range(heads):
        base = h * HEAD_DIM
        for j in range(3):
            pk[3 * heads, base + j] = 1.0
            pk[j * heads + h, base + 3 + j] = 1.0
            pq[base + j, j * heads + h] = 1.0
            pq[base + 3 + j, 3 * heads] = -1.0
    return dict(
        g1=norm1_g.astype(F32).reshape(1, d),
        wn_prompt=jnp.concatenate([wk, wu], axis=1).astype(BF16),
        wn_sample=jnp.concatenate([wq, wk, wv, wu], axis=1).astype(BF16),
        wf=wf, bf=bf, wt=jnp.concatenate([wq, wk, wv], axis=1).T.astype(BF16), wft=wft, bft=bft,
        pk=jnp.asarray(pk, BF16), pq=jnp.asarray(pq, BF16))


def _layer(xp, xs, cache_k, cache_v, cache_logf, st_re, st_im, page_table, w):
    b, l, d = xp.shape
    db, tnew, _ = xs.shape
    att = d // 2
    heads = att // HEAD_DIM
    sw = d - att
    g = sw // SSM_GROUP
    tm, tq, tk, pps = 512, 512, 256, 8

    pp = _proj_params(w["w_in"], w["b_f"], w["norm1_g"], d)
    bmat, cmat, lpr, lpi, dskip = _s5_params(w["a_re"], w["a_im"], w["log_dt"], w["b_re"], w["b_im"],
                                             w["c_re"], w["c_im"], w["d"])
    tail_w = (w["w_glu"].astype(BF16), None, w["norm_ssm_g"].astype(F32).reshape(1, sw),
              w["w_out"][:att].astype(BF16), w["w_out"][att:].astype(BF16),
              w["norm2_g"].astype(F32).reshape(1, d), w["w_gate"].astype(BF16), w["w_up"].astype(BF16),
              w["w_down"].astype(BF16), w["norm_f_g"].astype(F32).reshape(1, d))
    gatt = w["norm_attn_g"].astype(F32)
    pconst = (pp["g1"], None, pp["wf"], pp["bf"], pp["wt"], pp["wft"], pp["bft"], pp["pk"], pp["pq"])

    def consts(kind):
        c = list(pconst)
        c[1] = pp["wn_" + kind]
        return c

    up, kt, vt32, lft, ka, qta, vt = _proj(xp, *consts("prompt"), prompt=True, tm=tm, tk=tk)
    att_t = _fox(ka, qta, vt, tq=tq, tk=tk)
    yp_s, hrp, hip = _s5(up, bmat, cmat, lpr, lpi, dskip, nseq=b, tm=tm)
    tw = list(tail_w)
    tw[1] = gatt.reshape(att, 1)
    y_prompt = _tail(xp, att_t, yp_s, *tw, att_transposed=True, tm=tm)

    ts = db * tnew
    tms = min(tm, ts)
    us, qs, ks, vs, lfs, cnt = _proj(xs.reshape(1, ts, d), *consts("sample"), prompt=False, tm=tms, tk=tk)
    cnt = cnt.reshape(heads, db, tnew).transpose(1, 0, 2)
    bias = _pastbias(page_table, cache_logf.transpose(0, 2, 1))
    att_s = _decode(page_table, qs, ks, vs, cnt, bias, cache_k.transpose(0, 2, 3, 1),
                    cache_v.transpose(0, 2, 3, 1), tnew=tnew, pps=pps)
    h0 = (st_re.astype(F32).reshape(db, g * STATE_DIM), st_im.astype(F32).reshape(db, g * STATE_DIM))
    ys_s, hrs, his = _s5(us, bmat, cmat, lpr, lpi, dskip, h0, nseq=db, tm=tms)
    tw[1] = gatt.reshape(1, att)
    y_sample = _tail(xs.reshape(1, ts, d), att_s, ys_s, *tw, att_transposed=False, tm=tms)

    return (y_prompt, y_sample.reshape(db, tnew, d),
            kt.transpose(0, 3, 1, 2)[None], vt32.transpose(0, 3, 1, 2)[None],
            lft.transpose(0, 2, 1)[None],
            hrp.reshape(1, b, g, STATE_DIM), hip.reshape(1, b, g, STATE_DIM),
            ks.reshape(1, db, tnew, heads, HEAD_DIM), vs.reshape(1, db, tnew, heads, HEAD_DIM),
            lfs.reshape(1, db, tnew, heads),
            hrs.reshape(1, db, g, STATE_DIM), his.reshape(1, db, g, STATE_DIM))


def kernel(x_prompt, x_sample, cache_k, cache_v, cache_logf, state_ssm_re, state_ssm_im, page_table, norm1_g, w_in, b_f, ssm_a_re, ssm_a_im, ssm_log_dt, ssm_b_re, ssm_b_im, ssm_c_re, ssm_c_im, ssm_d, w_glu, norm_attn_g, norm_ssm_g, w_out, norm2_g, w_gate, w_up, w_down, norm_f_g):
    assert w_in.shape[0] == 1, "single-layer trunk"
    w = dict(norm1_g=norm1_g[0], w_in=w_in[0], b_f=b_f[0], a_re=ssm_a_re[0], a_im=ssm_a_im[0],
             log_dt=ssm_log_dt[0], b_re=ssm_b_re[0], b_im=ssm_b_im[0], c_re=ssm_c_re[0], c_im=ssm_c_im[0],
             d=ssm_d[0], w_glu=w_glu[0], norm_attn_g=norm_attn_g[0], norm_ssm_g=norm_ssm_g[0],
             w_out=w_out[0], norm2_g=norm2_g[0], w_gate=w_gate[0], w_up=w_up[0], w_down=w_down[0],
             norm_f_g=norm_f_g)
    return _layer(x_prompt, x_sample, cache_k[0], cache_v[0], cache_logf[0], state_ssm_re[0],
                  state_ssm_im[0], page_table, w)
```

```python
import functools
import math

import jax
import jax.numpy as jnp
import numpy as np
from jax import lax
from jax.experimental import pallas as pl
from jax.experimental.pallas import tpu as pltpu

F32 = jnp.float32
BF16 = jnp.bfloat16

HEAD_DIM = 64
SSM_GROUP = 16
STATE_DIM = 64
EPS = 1e-6
NEG = -1e30
LANES = 128
CHUNK = 8
VT_ROWS = HEAD_DIM + 16
LOG2E = math.log2(math.e)
VMEM_LIMIT = 56 * 1024 * 1024


def _dot(a, b):
    return jnp.dot(a, b, preferred_element_type=F32)


def _dot_nt(a, b):
    return lax.dot_general(a, b, (((1,), (1,)), ((), ())), preferred_element_type=F32)


def _dot_tn(a, b):
    return lax.dot_general(a, b, (((0,), (0,)), ((), ())), preferred_element_type=F32)


def _split3(x):
    hi = x.astype(BF16)
    r1 = x - hi.astype(F32)
    mid = r1.astype(BF16)
    lo = (r1 - mid.astype(F32)).astype(BF16)
    return hi, mid, lo


def _sum3(x, n, axis):
    if axis == 0:
        return x[:n] + x[n:2 * n] + x[2 * n:3 * n]
    return x[:, :n] + x[:, n:2 * n] + x[:, 2 * n:3 * n]


def _rms_rows(x, g):
    r = lax.rsqrt(jnp.mean(x * x, axis=-1, keepdims=True) + EPS)
    return x * r * g


def _log_sigmoid(x):
    return jnp.minimum(x, 0.0) - jnp.log1p(jnp.exp(-jnp.abs(x)))


def _params(sem, vmem=VMEM_LIMIT):
    return pltpu.CompilerParams(dimension_semantics=sem, vmem_limit_bytes=vmem)


def _const_spec(shape):
    nd = len(shape)
    return pl.BlockSpec(shape, lambda *_: (0,) * nd, pipeline_mode=pl.Buffered(1))


def _proj_kernel(x_ref, g_ref, wn_ref, wf_ref, bf_ref, wt_ref, wft_ref, bft_ref, pk_ref, pq_ref,
                 *out_and_scratch, prompt, tm, att, heads, tk):
    if prompt:
        (u_ref, kt_ref, vt32_ref, lft_ref, ka_ref, qta_ref, vt_ref, car_ref, cart_ref) = out_and_scratch
    else:
        (u_ref, q_ref, k_ref, v_ref, lf_ref, cnt_ref) = out_and_scratch
    i = pl.program_id(1)
    xn = _rms_rows(x_ref[...], g_ref[...]).astype(BF16)
    z = _dot(xn, wn_ref[...])
    if prompt:
        k, u = z[:, :att], z[:, att:]
    else:
        q, k, v, u = z[:, :att], z[:, att:2 * att], z[:, 2 * att:3 * att], z[:, 3 * att:]
        q_ref[...] = q
        k_ref[...] = k
        v_ref[...] = v
    for s in range(u.shape[1] // LANES):
        u_ref[s] = u[:, s * LANES:(s + 1) * LANES]
    lft = _log_sigmoid(_dot_nt(wft_ref[...], xn) + bft_ref[...])
    row = lax.broadcasted_iota(jnp.int32, (tm, tm), 0)
    col = lax.broadcasted_iota(jnp.int32, (tm, tm), 1)
    if prompt:
        triu = jnp.where(row <= col, 1.0, 0.0).astype(BF16)
    else:
        same = (row // CHUNK) == (col // CHUNK)
        triu = jnp.where(same & (row <= col), 1.0, 0.0).astype(BF16)
    hi, mid, lo = _split3(lft[:heads])
    zpad = jnp.zeros((heads, tm), BF16)
    ct = _sum3(_dot(jnp.concatenate([hi, mid, lo, zpad], axis=0), triu), heads, 0)
    if not prompt:
        lf_ref[...] = _log_sigmoid(_dot(xn, wf_ref[...]) + bf_ref[...])[:, :heads]
        cnt_ref[...] = ct
        return

    @pl.when(i == 0)
    def _():
        car_ref[...] = jnp.zeros_like(car_ref)
        cart_ref[...] = jnp.zeros_like(cart_ref)

    lft_ref[0] = lft[:heads]
    ct = ct + cart_ref[:, 0:1]
    cart_ref[...] = jnp.broadcast_to(ct[:, tm - 1:tm], cart_ref.shape)
    lf = _log_sigmoid(_dot(xn, wf_ref[...]) + bf_ref[...])
    tril = jnp.where(col <= row, 1.0, 0.0).astype(BF16)
    hi, mid, lo = _split3(lf)
    c = _sum3(_dot(tril, jnp.concatenate([hi, mid, lo], axis=1)), LANES, 1) + car_ref[...]
    car_ref[...] = c[tm - 1:tm, :]
    lane = lax.broadcasted_iota(jnp.int32, (tm, LANES), 1)
    chi, cmid, clo = _split3(c * LOG2E)
    cpack = jnp.where(lane < heads, chi,
                      jnp.where(lane < 2 * heads, cmid,
                                jnp.where(lane < 3 * heads, clo,
                                          jnp.where(lane == 3 * heads, 1.0, 0.0).astype(BF16))))
    kaug = _dot(cpack, pk_ref[...])
    zt = _dot_nt(wt_ref[...], xn)
    srow = lax.broadcasted_iota(jnp.int32, (4 * heads, tm), 0)
    thi, tmid, tlo = _split3(ct * LOG2E)
    ctpack = jnp.concatenate([thi, tmid, tlo, jnp.ones((heads, tm), BF16)], axis=0)
    ctpack = jnp.where(srow <= 3 * heads, ctpack, jnp.zeros_like(ctpack))
    qaug = _dot(pq_ref[...], ctpack)
    for h in range(heads):
        sl = slice(h * HEAD_DIM, (h + 1) * HEAD_DIM)
        kt = zt[att + h * HEAD_DIM:att + (h + 1) * HEAD_DIM, :]
        vt = zt[2 * att + h * HEAD_DIM:2 * att + (h + 1) * HEAD_DIM, :]
        kt_ref[0, h] = kt
        vt32_ref[0, h] = vt
        ka_ref[0, h, :, 0:HEAD_DIM] = k[:, sl].astype(BF16)
        ka_ref[0, h, :, HEAD_DIM:2 * HEAD_DIM] = kaug[:, sl].astype(BF16)
        qta_ref[0, h, 0:HEAD_DIM, :] = zt[sl, :].astype(BF16)
        qta_ref[0, h, HEAD_DIM:2 * HEAD_DIM, :] = qaug[sl, :].astype(BF16)
        for jj in range(tm // tk):
            vt_ref[0, h, jj, 0:HEAD_DIM, :] = vt[:, jj * tk:(jj + 1) * tk].astype(BF16)
            vt_ref[0, h, jj, HEAD_DIM:VT_ROWS, :] = jnp.ones((VT_ROWS - HEAD_DIM, tk), BF16)


def _proj(x, g1, wn, wf, bf, wt, wft, bft, pk, pq, *, prompt, tm, tk):
    b, l, d = x.shape
    att = d // 2
    heads = att // HEAD_DIM
    nslab = (d - att) // LANES
    t = b * l
    nb = l // tm
    x2 = x.reshape(t, d)
    tok = lambda bi, i: (bi * nb + i, 0)
    out_shape = [jax.ShapeDtypeStruct((nslab, t, LANES), F32)]
    out_specs = [pl.BlockSpec((nslab, tm, LANES), lambda bi, i: (0, bi * nb + i, 0))]
    scratch = []
    if prompt:
        hd_t = pl.BlockSpec((1, heads, HEAD_DIM, tm), lambda bi, i: (bi, 0, 0, i))
        out_shape += [jax.ShapeDtypeStruct((b, heads, HEAD_DIM, l), F32),
                      jax.ShapeDtypeStruct((b, heads, HEAD_DIM, l), F32),
                      jax.ShapeDtypeStruct((b, heads, l), F32),
                      jax.ShapeDtypeStruct((b, heads, l, 2 * HEAD_DIM), BF16),
                      jax.ShapeDtypeStruct((b, heads, 2 * HEAD_DIM, l), BF16),
                      jax.ShapeDtypeStruct((b, heads, l // tk, VT_ROWS, tk), BF16)]
        out_specs += [hd_t, hd_t,
                      pl.BlockSpec((1, heads, tm), lambda bi, i: (bi, 0, i)),
                      pl.BlockSpec((1, heads, tm, 2 * HEAD_DIM), lambda bi, i: (bi, 0, i, 0)),
                      pl.BlockSpec((1, heads, 2 * HEAD_DIM, tm), lambda bi, i: (bi, 0, 0, i)),
                      pl.BlockSpec((1, heads, tm // tk, VT_ROWS, tk), lambda bi, i: (bi, 0, i, 0, 0))]
        scratch = [pltpu.VMEM((1, LANES), F32), pltpu.VMEM((heads, LANES), F32)]
    else:
        out_shape += [jax.ShapeDtypeStruct((t, att), F32)] * 3 + [
            jax.ShapeDtypeStruct((t, heads), F32), jax.ShapeDtypeStruct((heads, t), F32)]
        out_specs += [pl.BlockSpec((tm, att), tok)] * 3 + [
            pl.BlockSpec((tm, heads), tok), pl.BlockSpec((heads, tm), lambda bi, i: (0, bi * nb + i))]
    consts = (g1, wn, wf, bf, wt, wft, bft, pk, pq)
    return pl.pallas_call(
        functools.partial(_proj_kernel, prompt=prompt, tm=tm, att=att, heads=heads, tk=tk),
        grid=(b, nb),
        in_specs=[pl.BlockSpec((tm, d), tok)] + [_const_spec(c.shape) for c in consts],
        out_specs=out_specs, out_shape=out_shape, scratch_shapes=scratch,
        compiler_params=_params(("arbitrary", "arbitrary")),
        name="proj_prompt" if prompt else "proj_sample",
    )(x2, *consts)


def _fox_kernel(ka_ref, qta_ref, vt_ref, o_ref, *, tq, tk, hpb):
    qi = pl.program_id(2)
    ratio = tq // tk
    qts = [qta_ref[0, hh] for hh in range(hpb)]

    def blocks(j, carries, masked):
        ss = [_dot(ka_ref[0, hh, j], qts[hh]) for hh in range(hpb)]
        ps, ms, alphas = [], [], []
        for hh in range(hpb):
            s = ss[hh] if masked is None else jnp.where(masked, ss[hh], NEG)
            m = carries[hh][0]
            m_new = jnp.maximum(m, jnp.max(s, axis=0, keepdims=True))
            ps.append(jnp.exp2(s - m_new).astype(BF16))
            alphas.append(jnp.exp2(m - m_new))
            ms.append(m_new)
        return tuple((ms[hh], alphas[hh] * carries[hh][1] + _dot(vt_ref[0, hh, j], ps[hh]))
                     for hh in range(hpb))

    init = tuple((jnp.full((1, tq), NEG, F32), jnp.zeros((VT_ROWS, tq), F32)) for _ in range(hpb))
    carries = lax.fori_loop(0, qi * ratio, lambda j, c: blocks(j, c, None), init)
    key = lax.broadcasted_iota(jnp.int32, (tk, tq), 0)
    qry = lax.broadcasted_iota(jnp.int32, (tk, tq), 1)
    for d in range(ratio):
        carries = blocks(qi * ratio + d, carries, key + d * tk <= qry)
    for hh in range(hpb):
        acc = carries[hh][1]
        o_ref[0, hh * HEAD_DIM:(hh + 1) * HEAD_DIM, :] = acc[:HEAD_DIM] / acc[HEAD_DIM:HEAD_DIM + 1]


def _fox(ka, qta, vt, *, tq, tk, hpb):
    b, heads, l, _ = ka.shape
    ka5 = ka.reshape(b, heads, l // tk, tk, 2 * HEAD_DIM)
    return pl.pallas_call(
        functools.partial(_fox_kernel, tq=tq, tk=tk, hpb=hpb),
        grid=(b, heads // hpb, l // tq),
        in_specs=[pl.BlockSpec((1, hpb, l // tk, tk, 2 * HEAD_DIM), lambda bi, h, i: (bi, h, 0, 0, 0),
                               pipeline_mode=pl.Buffered(1)),
                  pl.BlockSpec((1, hpb, 2 * HEAD_DIM, tq), lambda bi, h, i: (bi, h, 0, i)),
                  pl.BlockSpec((1, hpb, l // tk, VT_ROWS, tk), lambda bi, h, i: (bi, h, 0, 0, 0),
                               pipeline_mode=pl.Buffered(1))],
        out_specs=pl.BlockSpec((1, hpb * HEAD_DIM, tq), lambda bi, h, i: (bi, h, i)),
        out_shape=jax.ShapeDtypeStruct((b, heads * HEAD_DIM, l), F32),
        compiler_params=_params(("arbitrary", "arbitrary", "arbitrary")),
        name="fox_prompt",
    )(ka5, qta, vt)


def _pastbias_kernel(pt_ref, lf_hbm, umat_ref, later_ref, o_ref, xbuf, sems, *, npages, heads):
    b = pl.program_id(0)
    nb = pl.num_programs(0)
    slot = lax.rem(b, 2)
    rows = npages * heads

    def copy(bb, sl, p):
        return pltpu.make_async_copy(lf_hbm.at[pt_ref[bb, p]], xbuf.at[sl, pl.ds(p * heads, heads)],
                                     sems.at[sl, p])

    def start_all(bb, sl):
        for p in range(npages):
            copy(bb, sl, p).start()

    @pl.when(b == 0)
    def _():
        start_all(0, 0)

    @pl.when(b + 1 < nb)
    def _():
        start_all(b + 1, 1 - slot)

    for p in range(npages):
        copy(b, slot, p).wait()

    x = xbuf[slot]
    hi, mid, lo = _split3(x)
    both = _sum3(_dot(jnp.concatenate([hi, mid, lo], axis=0), umat_ref[...]), rows, 0)
    rin = both[:, :LANES]
    tot = both[:, LANES:]
    hi, mid, lo = _split3(tot)
    off = _sum3(_dot(later_ref[...], jnp.concatenate([hi, mid, lo], axis=1)), LANES, 1)
    o_ref[0] = rin + off


def _pastbias(page_table, lf_t):
    n_pool, heads, page = lf_t.shape
    db, npages = page_table.shape
    assert page == LANES
    rows = npages * heads
    s = np.arange(page)
    umat = np.concatenate([s[:, None] > s[None, :], np.ones((page, page), bool)], axis=1)
    r = np.arange(rows)
    later = (r[None, :] // heads > r[:, None] // heads) & (r[None, :] % heads == r[:, None] % heads)
    umat, later = jnp.asarray(umat, BF16), jnp.asarray(later, BF16)
    grid_spec = pltpu.PrefetchScalarGridSpec(
        num_scalar_prefetch=1, grid=(db,),
        in_specs=[pl.BlockSpec(memory_space=pl.ANY),
                  pl.BlockSpec(umat.shape, lambda b, pt: (0, 0)),
                  pl.BlockSpec(later.shape, lambda b, pt: (0, 0))],
        out_specs=pl.BlockSpec((1, rows, page), lambda b, pt: (b, 0, 0)),
        scratch_shapes=[pltpu.VMEM((2, rows, page), F32), pltpu.SemaphoreType.DMA((2, npages))])
    return pl.pallas_call(
        functools.partial(_pastbias_kernel, npages=npages, heads=heads),
        grid_spec=grid_spec,
        out_shape=jax.ShapeDtypeStruct((db, rows, page), F32),
        compiler_params=_params(("arbitrary",)),
        name="pastbias",
    )(page_table, lf_t, umat, later)


def _decode_kernel(pt_ref, q_ref, kn_ref, vn_ref, cnt_ref, bias_ref, *rest, pps, heads, tnew):
    k_refs = rest[:pps]
    v_refs = rest[pps:2 * pps]
    o_ref, qbd_ref, m_ref, l_ref, acc_ref = rest[2 * pps:]
    s = pl.program_id(1)
    ns = pl.num_programs(1)
    rows = heads * tnew
    width = heads * HEAD_DIM
    page = k_refs[0].shape[-1]
    rhead = lax.broadcasted_iota(jnp.int32, (rows, width), 0) // tnew
    chead = lax.broadcasted_iota(jnp.int32, (rows, width), 1) // HEAD_DIM

    @pl.when(s == 0)
    def _():
        qtile = jnp.concatenate([q_ref[...]] * heads, axis=0)
        qbd_ref[...] = jnp.where(rhead == chead, qtile, 0.0).astype(BF16)
        m_ref[...] = jnp.full_like(m_ref, NEG)
        l_ref[...] = jnp.zeros_like(l_ref)
        acc_ref[...] = jnp.zeros_like(acc_ref)

    qbd = qbd_ref[...]

    def cat_pages(refs, g):
        return jnp.concatenate([refs[2 * g + e][0].reshape(width, page).astype(BF16) for e in range(2)], axis=1)

    scores = []
    for g in range(pps // 2):
        bias = jnp.concatenate(
            [jnp.concatenate([jnp.broadcast_to(bias_ref[0, pl.ds((2 * g + e) * heads + h, 1), :], (tnew, page))
                              for h in range(heads)], axis=0) for e in range(2)], axis=1)
        scores.append(_dot(qbd, cat_pages(k_refs, g)) + bias)
    m_old = m_ref[...]
    m_new = m_old
    for st in scores:
        m_new = jnp.maximum(m_new, jnp.max(st, axis=1, keepdims=True))
    alpha = jnp.exp(m_old - m_new)
    l = alpha * l_ref[...]
    acc = alpha * acc_ref[...]
    for g in range(pps // 2):
        p = jnp.exp(scores[g] - m_new)
        l = l + jnp.sum(p, axis=1, keepdims=True)
        acc = acc + _dot_nt(p.astype(BF16), cat_pages(v_refs, g))
    m_ref[...] = m_new
    l_ref[...] = l
    acc_ref[...] = acc

    @pl.when(s == ns - 1)
    def _():
        st = _dot_nt(qbd, kn_ref[...].astype(BF16))
        bias = jnp.concatenate(
            [jnp.broadcast_to(cnt_ref[0, h:h + 1, :], (tnew, tnew)) for h in range(heads)], axis=0)
        qi = lax.broadcasted_iota(jnp.int32, (rows, tnew), 0) % tnew
        kj = lax.broadcasted_iota(jnp.int32, (rows, tnew), 1)
        st = jnp.where(kj <= qi, st - bias, NEG)
        m_fin = jnp.maximum(m_new, jnp.max(st, axis=1, keepdims=True))
        a2 = jnp.exp(m_new - m_fin)
        p = jnp.exp(st - m_fin)
        lf = a2 * l + jnp.sum(p, axis=1, keepdims=True)
        accf = a2 * acc + _dot(p.astype(BF16), vn_ref[...].astype(BF16))
        res = jnp.where(rhead == chead, accf / lf, 0.0)
        out = res[0:tnew]
        for h in range(1, heads):
            out = out + res[h * tnew:(h + 1) * tnew]
        o_ref[...] = out


def _decode(page_table, q, kn, vn, cnt, bias, ck_t, cv_t, *, tnew, pps):
    n_pool, heads, hd, page = ck_t.shape
    db, npages = page_table.shape
    width = heads * hd
    rows = heads * tnew
    assert pps % 2 == 0 and npages % pps == 0
    tokspec = pl.BlockSpec((tnew, width), lambda b, s, pt: (b, 0))

    def page_spec(i):
        return pl.BlockSpec((1, heads, hd, page), lambda b, s, pt: (pt[b, s * pps + i], 0, 0, 0))

    grid_spec = pltpu.PrefetchScalarGridSpec(
        num_scalar_prefetch=1, grid=(db, npages // pps),
        in_specs=[tokspec, tokspec, tokspec,
                  pl.BlockSpec((1, heads, tnew), lambda b, s, pt: (b, 0, 0)),
                  pl.BlockSpec((1, pps * heads, page), lambda b, s, pt: (b, s, 0))]
                 + [page_spec(i) for i in range(pps)] * 2,
        out_specs=tokspec,
        scratch_shapes=[pltpu.VMEM((rows, width), BF16), pltpu.VMEM((rows, 1), F32),
                        pltpu.VMEM((rows, 1), F32), pltpu.VMEM((rows, width), F32)])
    return pl.pallas_call(
        functools.partial(_decode_kernel, pps=pps, heads=heads, tnew=tnew),
        grid_spec=grid_spec,
        out_shape=jax.ShapeDtypeStruct((db * tnew, width), F32),
        compiler_params=_params(("arbitrary", "arbitrary")),
        name="decode_sample",
    )(page_table, q, kn, vn, cnt, bias, *([ck_t] * pps), *([cv_t] * pps))


def _s5_kernel(u_ref, b_ref, c_ref, lpr_ref, lpi_ref, d_ref, *rest, has_init, n, half):
    if has_init:
        h0r_ref, h0i_ref, y_ref, hr_ref, hi_ref, hlr, hli = rest
    else:
        y_ref, hr_ref, hi_ref, hlr, hli, hpr, hpi, car, cai = rest
    nslab = u_ref.shape[0]
    hs = nslab // 2

    def u_rows(j):
        return [u_ref[s, pl.ds(j, n, stride=CHUNK), :] for s in range(nslab)]

    lr1, li1 = lpr_ref[0:1, :], lpi_ref[0:1, :]
    hr = hi = None
    for j in range(CHUNK):
        us = u_rows(j)
        ua = jnp.concatenate(us[:hs], axis=1).astype(BF16)
        ub = jnp.concatenate(us[hs:], axis=1).astype(BF16)
        ba = _dot(ua, b_ref[0])
        bb = _dot(ub, b_ref[1])
        bur = jnp.concatenate([ba[:, :half], bb[:, :half]], axis=1)
        bui = jnp.concatenate([ba[:, half:], bb[:, half:]], axis=1)
        if hr is None:
            hr, hi = bur, bui
        else:
            hr, hi = lr1 * hr - li1 * hi + bur, lr1 * hi + li1 * hr + bui
        hlr[j] = hr
        hli[j] = hi

    if has_init:
        pr_all, pi_all = h0r_ref[...], h0i_ref[...]
    else:
        i = pl.program_id(1)

        @pl.when(i == 0)
        def _():
            car[...] = jnp.zeros_like(car)
            cai[...] = jnp.zeros_like(cai)

        ltr, lti = lpr_ref[CHUNK - 1:CHUNK, :], lpi_ref[CHUNK - 1:CHUNK, :]

        def step(c, carry):
            pr, pi_ = carry
            hpr[pl.ds(c, 1), :] = pr
            hpi[pl.ds(c, 1), :] = pi_
            er = hlr[CHUNK - 1, pl.ds(c, 1), :]
            ei = hli[CHUNK - 1, pl.ds(c, 1), :]
            return ltr * pr - lti * pi_ + er, ltr * pi_ + lti * pr + ei

        pr, pi_ = lax.fori_loop(0, n, step, (car[...], cai[...]))
        car[...] = pr
        cai[...] = pi_
        hr_ref[0] = pr
        hi_ref[0] = pi_
        pr_all, pi_all = hpr[...], hpi[...]

    for j in range(CHUNK):
        lr, li = lpr_ref[j:j + 1, :], lpi_ref[j:j + 1, :]
        fr = hlr[j] + lr * pr_all - li * pi_all
        fi = hli[j] + lr * pi_all + li * pr_all
        if has_init and j == CHUNK - 1:
            hr_ref[...] = fr
            hi_ref[...] = fi
        la = jnp.concatenate([fr[:, :half], fi[:, :half]], axis=1).astype(BF16)
        lb = jnp.concatenate([fr[:, half:], fi[:, half:]], axis=1).astype(BF16)
        y = jnp.concatenate([_dot(la, c_ref[0]), _dot(lb, c_ref[1])], axis=1)
        us = u_rows(j)
        for s in range(nslab):
            y_ref[s, pl.ds(j, n, stride=CHUNK), :] = (
                y[:, s * LANES:(s + 1) * LANES] + d_ref[:, s * LANES:(s + 1) * LANES] * us[s])


def _s5(u_slabs, bmat, cmat, lpr, lpi, dskip, h0=None, *, nseq, tm):
    nslab, t, _ = u_slabs.shape
    l = t // nseq
    n = tm // CHUNK
    sw = lpr.shape[1]
    half = sw // 2
    has_init = h0 is not None
    consts = (bmat, cmat, lpr, lpi, dskip)
    scratch = [pltpu.VMEM((CHUNK, n, sw), F32), pltpu.VMEM((CHUNK, n, sw), F32)]
    if has_init:
        nb = t // tm
        grid = (1, nb)
        slab_spec = pl.BlockSpec((nslab, tm, LANES), lambda bi, i: (0, i, 0))
        st_spec = pl.BlockSpec((n, sw), lambda bi, i: (i, 0))
        ins = [u_slabs, *consts, h0[0], h0[1]]
        in_specs = [slab_spec] + [_const_spec(c.shape) for c in consts] + [st_spec, st_spec]
        st_shape = jax.ShapeDtypeStruct((t // CHUNK, sw), F32)
    else:
        nb = l // tm
        grid = (nseq, nb)
        slab_spec = pl.BlockSpec((nslab, tm, LANES), lambda bi, i: (0, bi * nb + i, 0))
        st_spec = pl.BlockSpec((1, 1, sw), lambda bi, i: (bi, 0, 0))
        ins = [u_slabs, *consts]
        in_specs = [slab_spec] + [_const_spec(c.shape) for c in consts]
        st_shape = jax.ShapeDtypeStruct((nseq, 1, sw), F32)
        scratch += [pltpu.VMEM((n, sw), F32), pltpu.VMEM((n, sw), F32),
                    pltpu.VMEM((1, sw), F32), pltpu.VMEM((1, sw), F32)]
    return pl.pallas_call(
        functools.partial(_s5_kernel, has_init=has_init, n=n, half=half),
        grid=grid, in_specs=in_specs,
        out_specs=[slab_spec, st_spec, st_spec],
        out_shape=[jax.ShapeDtypeStruct((nslab, t, LANES), F32), st_shape, st_shape],
        scratch_shapes=scratch,
        compiler_params=_params(("arbitrary", "arbitrary")),
        name="s5_sample" if has_init else "s5_prompt",
    )(*ins)


def _gelu_tanh(x):
    return x * (0.5 * (1.0 + jnp.tanh(math.sqrt(2.0 / math.pi) * (x + 0.044715 * (x * x * x)))))


def _tail_kernel(x_ref, att_ref, y_ref, wglu_ref, gatt_ref, gssm_ref, woa_ref, wos_ref, g2_ref,
                 wg_ref, wu_ref, wd_ref, gf_ref, o_ref, *, att_transposed, sw):
    x = x_ref[...]
    yv = jnp.concatenate([y_ref[s] for s in range(y_ref.shape[0])], axis=1)
    gv = _dot(_gelu_tanh(yv).astype(BF16), wglu_ref[...])
    ssm = gv[:, :sw] * jax.nn.sigmoid(gv[:, sw:])
    ssm_n = _rms_rows(ssm, gssm_ref[...]).astype(BF16)
    if att_transposed:
        at = att_ref[0]
        r = lax.rsqrt(jnp.mean(at * at, axis=0, keepdims=True) + EPS)
        att_n = (at * r * gatt_ref[...]).astype(BF16)
        ho = _dot_tn(att_n, woa_ref[...])
    else:
        ho = _dot(_rms_rows(att_ref[...], gatt_ref[...]).astype(BF16), woa_ref[...])
    h = x + ho + _dot(ssm_n, wos_ref[...])
    hn = _rms_rows(h, g2_ref[...]).astype(BF16)
    gt = _dot(hn, wg_ref[...])
    act = (gt * jax.nn.sigmoid(gt) * _dot(hn, wu_ref[...])).astype(BF16)
    y = h + _dot(act, wd_ref[...])
    o_ref[...] = _rms_rows(y, gf_ref[...])


def _tail(x, att, y_slabs, wglu, gatt, gssm, woa, wos, g2, wg, wu, wd, gf, *, att_transposed, tm):
    b, l, d = x.shape
    t = b * l
    nb = l // tm
    nslab = y_slabs.shape[0]
    sw = nslab * LANES
    tok = lambda bi, i: (bi * nb + i, 0)
    if att_transposed:
        att_spec = pl.BlockSpec((1, att.shape[1], tm), lambda bi, i: (bi, 0, i))
    else:
        att_spec = pl.BlockSpec((tm, att.shape[1]), tok)
    consts_a = (wglu, gatt, gssm, woa, wos, g2, wg, wu, wd, gf)
    out = pl.pallas_call(
        functools.partial(_tail_kernel, att_transposed=att_transposed, sw=sw),
        grid=(b, nb),
        in_specs=[pl.BlockSpec((tm, d), tok), att_spec,
                  pl.BlockSpec((nslab, tm, LANES), lambda bi, i: (0, bi * nb + i, 0))]
                 + [_const_spec(c.shape) for c in consts_a],
        out_specs=pl.BlockSpec((tm, d), tok),
        out_shape=jax.ShapeDtypeStruct((t, d), F32),
        compiler_params=_params(("arbitrary", "arbitrary")),
        name="tail_prompt" if att_transposed else "tail_sample",
    )(x.reshape(t, d), att, y_slabs, *consts_a)
    return out.reshape(b, l, d)


def _blockdiag(m):
    g, r, c = m.shape
    eye = jnp.eye(g, dtype=m.dtype)
    return (m[:, :, None, :] * eye[:, None, :, None]).reshape(g * r, g * c)


def _s5_params(a_re, a_im, log_dt, b_re, b_im, c_re, c_im, d):
    ar, ai = a_re.astype(F32), a_im.astype(F32)
    g, p = ar.shape
    dt = jnp.exp(log_dt.astype(F32))[:, None]
    mag = jnp.exp(ar * dt)
    lam_re = mag * jnp.cos(ai * dt)
    lam_im = mag * jnp.sin(ai * dt)
    den = ar * ar + ai * ai
    q_re = ((lam_re - 1.0) * ar + lam_im * ai) / den
    q_im = (lam_im * ar - (lam_re - 1.0) * ai) / den
    br, bi = b_re.astype(F32), b_im.astype(F32)
    bb_re = (q_re[..., None] * br - q_im[..., None] * bi).transpose(0, 2, 1)
    bb_im = (q_re[..., None] * bi + q_im[..., None] * br).transpose(0, 2, 1)
    gh = g // 2
    bmat = jnp.stack([jnp.concatenate([_blockdiag(bb_re[s]), _blockdiag(bb_im[s])], axis=1)
                      for s in (slice(0, gh), slice(gh, g))]).astype(BF16)
    cr = c_re.astype(F32).transpose(0, 2, 1)
    ci = c_im.astype(F32).transpose(0, 2, 1)
    cmat = jnp.stack([jnp.concatenate([_blockdiag(cr[s]), -_blockdiag(ci[s])], axis=0)
                      for s in (slice(0, gh), slice(gh, g))]).astype(BF16)
    pr, pi_ = [lam_re], [lam_im]
    for _ in range(CHUNK - 1):
        pr, pi_ = (pr + [pr[-1] * lam_re - pi_[-1] * lam_im], pi_ + [pr[-1] * lam_im + pi_[-1] * lam_re])
    lpr = jnp.stack(pr).reshape(CHUNK, g * p)
    lpi = jnp.stack(pi_).reshape(CHUNK, g * p)
    return bmat, cmat, lpr, lpi, d.astype(F32).reshape(1, -1)


def _proj_params(w_in, b_f, norm1_g, d):
    att = d // 2
    heads = att // HEAD_DIM
    scale = HEAD_DIM ** -0.5
    w = w_in.astype(F32)
    wq = w[:, :att] * scale
    wk, wv = w[:, att:2 * att], w[:, 2 * att:3 * att]
    wfl = w[:, 3 * att:3 * att + heads]
    wu = w[:, 3 * att + heads:]
    wf = jnp.zeros((d, LANES), F32).at[:, :3 * heads].set(jnp.tile(wfl, (1, 3))).astype(BF16)
    bf = jnp.zeros((1, LANES), F32).at[0, :3 * heads].set(jnp.tile(b_f.astype(F32), 3))
    wft = jnp.zeros((2 * heads, d), F32).at[:heads].set(wfl.T).astype(BF16)
    bft = jnp.zeros((2 * heads, 1), F32).at[:heads, 0].set(b_f.astype(F32))
    pk = np.zeros((LANES, att), np.float32)
    pq = np.zeros((att, 4 * heads), np.float32)
    for h in range(heads):
        base = h * HEAD_DIM
        for j in range(3):
            pk[3 * heads, base + j] = 1.0
            pk[j * heads + h, base + 3 + j] = 1.0
            pq[base + j, j * heads + h] = 1.0
            pq[base + 3 + j, 3 * heads] = -1.0
    return dict(
        g1=norm1_g.astype(F32).reshape(1, d),
        wn_prompt=jnp.concatenate([wk, wu], axis=1).astype(BF16),
        wn_sample=jnp.concatenate([wq, wk, wv, wu], axis=1).astype(BF16),
        wf=wf, bf=bf, wt=jnp.concatenate([wq * LOG2E, wk, wv], axis=1).T.astype(BF16), wft=wft, bft=bft,
        pk=jnp.asarray(pk, BF16), pq=jnp.asarray(pq, BF16))


def _layer(xp, xs, cache_k, cache_v, cache_logf, st_re, st_im, page_table, w):
    b, l, d = xp.shape
    db, tnew, _ = xs.shape
    att = d // 2
    heads = att // HEAD_DIM
    sw = d - att
    g = sw // SSM_GROUP
    tm, tq, tk, hpb = 512, 512, 512, 4
    tm_s5 = min(1024, l)
    pps = min(8, page_table.shape[1])

    pp = _proj_params(w["w_in"], w["b_f"], w["norm1_g"], d)
    bmat, cmat, lpr, lpi, dskip = _s5_params(w["a_re"], w["a_im"], w["log_dt"], w["b_re"], w["b_im"],
                                             w["c_re"], w["c_im"], w["d"])
    tail_w = (w["w_glu"].astype(BF16), None, w["norm_ssm_g"].astype(F32).reshape(1, sw),
              w["w_out"][:att].astype(BF16), w["w_out"][att:].astype(BF16),
              w["norm2_g"].astype(F32).reshape(1, d), w["w_gate"].astype(BF16), w["w_up"].astype(BF16),
              w["w_down"].astype(BF16), w["norm_f_g"].astype(F32).reshape(1, d))
    gatt = w["norm_attn_g"].astype(F32)
    pconst = (pp["g1"], None, pp["wf"], pp["bf"], pp["wt"], pp["wft"], pp["bft"], pp["pk"], pp["pq"])

    def consts(kind):
        c = list(pconst)
        c[1] = pp["wn_" + kind]
        return c

    up, kt, vt32, lft, ka, qta, vt = _proj(xp, *consts("prompt"), prompt=True, tm=tm, tk=tk)
    att_t = _fox(ka, qta, vt, tq=tq, tk=tk, hpb=hpb)
    yp_s, hrp, hip = _s5(up, bmat, cmat, lpr, lpi, dskip, nseq=b, tm=tm_s5)
    tw = list(tail_w)
    tw[1] = gatt.reshape(att, 1)
    y_prompt = _tail(xp, att_t, yp_s, *tw, att_transposed=True, tm=tm)

    ts = db * tnew
    tms = min(tm, ts)
    us, qs, ks, vs, lfs, cnt = _proj(xs.reshape(1, ts, d), *consts("sample"), prompt=False, tm=tms, tk=tk)
    cnt = cnt.reshape(heads, db, tnew).transpose(1, 0, 2)
    bias = _pastbias(page_table, cache_logf.transpose(0, 2, 1))
    att_s = _decode(page_table, qs, ks, vs, cnt, bias, cache_k.transpose(0, 2, 3, 1),
                    cache_v.transpose(0, 2, 3, 1), tnew=tnew, pps=pps)
    h0 = (st_re.astype(F32).reshape(db, g * STATE_DIM), st_im.astype(F32).reshape(db, g * STATE_DIM))
    ys_s, hrs, his = _s5(us, bmat, cmat, lpr, lpi, dskip, h0, nseq=db, tm=tms)
    tw[1] = gatt.reshape(1, att)
    y_sample = _tail(xs.reshape(1, ts, d), att_s, ys_s, *tw, att_transposed=False, tm=tms)

    return (y_prompt, y_sample.reshape(db, tnew, d),
            kt.transpose(0, 3, 1, 2)[None], vt32.transpose(0, 3, 1, 2)[None],
            lft.transpose(0, 2, 1)[None],
            hrp.reshape(1, b, g, STATE_DIM), hip.reshape(1, b, g, STATE_DIM),
            ks.reshape(1, db, tnew, heads, HEAD_DIM), vs.reshape(1, db, tnew, heads, HEAD_DIM),
            lfs.reshape(1, db, tnew, heads),
            hrs.reshape(1, db, g, STATE_DIM), his.reshape(1, db, g, STATE_DIM))


def kernel(x_prompt, x_sample, cache_k, cache_v, cache_logf, state_ssm_re, state_ssm_im, page_table, norm1_g, w_in, b_f, ssm_a_re, ssm_a_im, ssm_log_dt, ssm_b_re, ssm_b_im, ssm_c_re, ssm_c_im, ssm_d, w_glu, norm_attn_g, norm_ssm_g, w_out, norm2_g, w_gate, w_up, w_down, norm_f_g):
    assert w_in.shape[0] == 1, "single-layer trunk"
    w = dict(norm1_g=norm1_g[0], w_in=w_in[0], b_f=b_f[0], a_re=ssm_a_re[0], a_im=ssm_a_im[0],
             log_dt=ssm_log_dt[0], b_re=ssm_b_re[0], b_im=ssm_b_im[0], c_re=ssm_c_re[0], c_im=ssm_c_im[0],
             d=ssm_d[0], w_glu=w_glu[0], norm_attn_g=norm_attn_g[0], norm_ssm_g=norm_ssm_g[0],
             w_out=w_out[0], norm2_g=norm2_g[0], w_gate=w_gate[0], w_up=w_up[0], w_down=w_down[0],
             norm_f_g=norm_f_g)
    return _layer(x_prompt, x_sample, cache_k[0], cache_v[0], cache_logf[0], state_ssm_re[0],
                  state_ssm_im[0], page_table, w)
```

```python
import functools
import math

import jax
import jax.numpy as jnp
import numpy as np
from jax import lax
from jax.experimental import pallas as pl
from jax.experimental.pallas import tpu as pltpu

F32 = jnp.float32
BF16 = jnp.bfloat16

HEAD_DIM = 64
SSM_GROUP = 16
STATE_DIM = 64
EPS = 1e-6
NEG = -1e30
LANES = 128
CHUNK = 8
VT_ROWS = HEAD_DIM + 16
LOG2E = math.log2(math.e)
VMEM_LIMIT = 56 * 1024 * 1024


def _dot(a, b):
    return jnp.dot(a, b, preferred_element_type=F32)


def _dot_nt(a, b):
    return lax.dot_general(a, b, (((1,), (1,)), ((), ())), preferred_element_type=F32)


def _dot_tn(a, b):
    return lax.dot_general(a, b, (((0,), (0,)), ((), ())), preferred_element_type=F32)


def _split3(x):
    hi = x.astype(BF16)
    r1 = x - hi.astype(F32)
    mid = r1.astype(BF16)
    lo = (r1 - mid.astype(F32)).astype(BF16)
    return hi, mid, lo


def _sum3(x, n, axis):
    if axis == 0:
        return x[:n] + x[n:2 * n] + x[2 * n:3 * n]
    return x[:, :n] + x[:, n:2 * n] + x[:, 2 * n:3 * n]


def _rms_rows(x, g):
    r = lax.rsqrt(jnp.mean(x * x, axis=-1, keepdims=True) + EPS)
    return x * r * g


def _log_sigmoid(x):
    return jnp.minimum(x, 0.0) - jnp.log1p(jnp.exp(-jnp.abs(x)))


def _params(sem, vmem=VMEM_LIMIT):
    return pltpu.CompilerParams(dimension_semantics=sem, vmem_limit_bytes=vmem)


def _const_spec(shape):
    nd = len(shape)
    return pl.BlockSpec(shape, lambda *_: (0,) * nd, pipeline_mode=pl.Buffered(1))


def _proj_kernel(x_ref, g_ref, wn_ref, wf_ref, bf_ref, wt_ref, wft_ref, bft_ref, pk_ref, pq_ref,
                 *out_and_scratch, prompt, tm, att, heads, tk):
    if prompt:
        (u_ref, kt_ref, vt32_ref, lft_ref, ka_ref, qta_ref, vt_ref, car_ref, cart_ref) = out_and_scratch
    else:
        (u_ref, q_ref, k_ref, v_ref, lf_ref, cnt_ref) = out_and_scratch
    i = pl.program_id(1)
    xn = _rms_rows(x_ref[...], g_ref[...]).astype(BF16)
    z = _dot(xn, wn_ref[...])
    if prompt:
        k, u = z[:, :att], z[:, att:]
    else:
        q, k, v, u = z[:, :att], z[:, att:2 * att], z[:, 2 * att:3 * att], z[:, 3 * att:]
        q_ref[...] = q
        k_ref[...] = k
        v_ref[...] = v
    for s in range(u.shape[1] // LANES):
        u_ref[s] = u[:, s * LANES:(s + 1) * LANES]
    lft = _log_sigmoid(_dot_nt(wft_ref[...], xn) + bft_ref[...])
    row = lax.broadcasted_iota(jnp.int32, (tm, tm), 0)
    col = lax.broadcasted_iota(jnp.int32, (tm, tm), 1)
    if prompt:
        triu = jnp.where(row <= col, 1.0, 0.0).astype(BF16)
    else:
        same = (row // CHUNK) == (col // CHUNK)
        triu = jnp.where(same & (row <= col), 1.0, 0.0).astype(BF16)
    hi, mid, lo = _split3(lft[:heads])
    zpad = jnp.zeros((heads, tm), BF16)
    ct = _sum3(_dot(jnp.concatenate([hi, mid, lo, zpad], axis=0), triu), heads, 0)
    if not prompt:
        lf_ref[...] = _log_sigmoid(_dot(xn, wf_ref[...]) + bf_ref[...])[:, :heads]
        cnt_ref[...] = ct
        return

    @pl.when(i == 0)
    def _():
        car_ref[...] = jnp.zeros_like(car_ref)
        cart_ref[...] = jnp.zeros_like(cart_ref)

    lft_ref[0] = lft[:heads]
    ct = ct + cart_ref[:, 0:1]
    cart_ref[...] = jnp.broadcast_to(ct[:, tm - 1:tm], cart_ref.shape)
    lf = _log_sigmoid(_dot(xn, wf_ref[...]) + bf_ref[...])
    tril = jnp.where(col <= row, 1.0, 0.0).astype(BF16)
    hi, mid, lo = _split3(lf)
    c = _sum3(_dot(tril, jnp.concatenate([hi, mid, lo], axis=1)), LANES, 1) + car_ref[...]
    car_ref[...] = c[tm - 1:tm, :]
    lane = lax.broadcasted_iota(jnp.int32, (tm, LANES), 1)
    chi, cmid, clo = _split3(c * LOG2E)
    cpack = jnp.where(lane < heads, chi,
                      jnp.where(lane < 2 * heads, cmid,
                                jnp.where(lane < 3 * heads, clo,
                                          jnp.where(lane == 3 * heads, 1.0, 0.0).astype(BF16))))
    kaug = _dot(cpack, pk_ref[...])
    zt = _dot_nt(wt_ref[...], xn)
    srow = lax.broadcasted_iota(jnp.int32, (4 * heads, tm), 0)
    thi, tmid, tlo = _split3(ct * LOG2E)
    ctpack = jnp.concatenate([thi, tmid, tlo, jnp.ones((heads, tm), BF16)], axis=0)
    ctpack = jnp.where(srow <= 3 * heads, ctpack, jnp.zeros_like(ctpack))
    qaug = _dot(pq_ref[...], ctpack)
    for h in range(heads):
        sl = slice(h * HEAD_DIM, (h + 1) * HEAD_DIM)
        kt = zt[att + h * HEAD_DIM:att + (h + 1) * HEAD_DIM, :]
        vt = zt[2 * att + h * HEAD_DIM:2 * att + (h + 1) * HEAD_DIM, :]
        kt_ref[0, h] = kt
        vt32_ref[0, h] = vt
        ka_ref[0, h, :, 0:HEAD_DIM] = k[:, sl].astype(BF16)
        ka_ref[0, h, :, HEAD_DIM:2 * HEAD_DIM] = kaug[:, sl].astype(BF16)
        qta_ref[0, h, 0:HEAD_DIM, :] = zt[sl, :].astype(BF16)
        qta_ref[0, h, HEAD_DIM:2 * HEAD_DIM, :] = qaug[sl, :].astype(BF16)
        for jj in range(tm // tk):
            vt_ref[0, h, jj, 0:HEAD_DIM, :] = vt[:, jj * tk:(jj + 1) * tk].astype(BF16)
            vt_ref[0, h, jj, HEAD_DIM:VT_ROWS, :] = jnp.ones((VT_ROWS - HEAD_DIM, tk), BF16)


def _proj(x, g1, wn, wf, bf, wt, wft, bft, pk, pq, *, prompt, tm, tk):
    b, l, d = x.shape
    att = d // 2
    heads = att // HEAD_DIM
    nslab = (d - att) // LANES
    t = b * l
    nb = l // tm
    x2 = x.reshape(t, d)
    tok = lambda bi, i: (bi * nb + i, 0)
    out_shape = [jax.ShapeDtypeStruct((nslab, t, LANES), F32)]
    out_specs = [pl.BlockSpec((nslab, tm, LANES), lambda bi, i: (0, bi * nb + i, 0))]
    scratch = []
    if prompt:
        hd_t = pl.BlockSpec((1, heads, HEAD_DIM, tm), lambda bi, i: (bi, 0, 0, i))
        out_shape += [jax.ShapeDtypeStruct((b, heads, HEAD_DIM, l), F32),
                      jax.ShapeDtypeStruct((b, heads, HEAD_DIM, l), F32),
                      jax.ShapeDtypeStruct((b, heads, l), F32),
                      jax.ShapeDtypeStruct((b, heads, l, 2 * HEAD_DIM), BF16),
                      jax.ShapeDtypeStruct((b, heads, 2 * HEAD_DIM, l), BF16),
                      jax.ShapeDtypeStruct((b, heads, l // tk, VT_ROWS, tk), BF16)]
        out_specs += [hd_t, hd_t,
                      pl.BlockSpec((1, heads, tm), lambda bi, i: (bi, 0, i)),
                      pl.BlockSpec((1, heads, tm, 2 * HEAD_DIM), lambda bi, i: (bi, 0, i, 0)),
                      pl.BlockSpec((1, heads, 2 * HEAD_DIM, tm), lambda bi, i: (bi, 0, 0, i)),
                      pl.BlockSpec((1, heads, tm // tk, VT_ROWS, tk), lambda bi, i: (bi, 0, i, 0, 0))]
        scratch = [pltpu.VMEM((1, LANES), F32), pltpu.VMEM((heads, LANES), F32)]
    else:
        out_shape += [jax.ShapeDtypeStruct((t, att), F32)] * 3 + [
            jax.ShapeDtypeStruct((t, heads), F32), jax.ShapeDtypeStruct((heads, t), F32)]
        out_specs += [pl.BlockSpec((tm, att), tok)] * 3 + [
            pl.BlockSpec((tm, heads), tok), pl.BlockSpec((heads, tm), lambda bi, i: (0, bi * nb + i))]
    consts = (g1, wn, wf, bf, wt, wft, bft, pk, pq)
    return pl.pallas_call(
        functools.partial(_proj_kernel, prompt=prompt, tm=tm, att=att, heads=heads, tk=tk),
        grid=(b, nb),
        in_specs=[pl.BlockSpec((tm, d), tok)] + [_const_spec(c.shape) for c in consts],
        out_specs=out_specs, out_shape=out_shape, scratch_shapes=scratch,
        compiler_params=_params(("arbitrary", "arbitrary")),
        name="proj_prompt" if prompt else "proj_sample",
    )(x2, *consts)


def _fox_kernel(ka_ref, qta_ref, vt_ref, o_ref, s0_ref, s1_ref, *, tq, tk, hpb):
    assert tq == 2 * tk
    qi = pl.program_id(2)
    qts = [qta_ref[0, hh] for hh in range(hpb)]

    def produce(j, buf):
        for hh in range(hpb):
            buf[hh] = _dot(ka_ref[0, hh, j], qts[hh])

    def consume(j, buf, carries, masked):
        ps, ms, alphas = [], [], []
        for hh in range(hpb):
            s = buf[hh] if masked is None else jnp.where(masked, buf[hh], NEG)
            m = carries[hh][0]
            m_new = jnp.maximum(m, jnp.max(s, axis=0, keepdims=True))
            ps.append(jnp.exp2(s - m_new).astype(BF16))
            alphas.append(jnp.exp2(m - m_new))
            ms.append(m_new)
        return tuple((ms[hh], alphas[hh] * carries[hh][1] + _dot(vt_ref[0, hh, j], ps[hh]))
                     for hh in range(hpb))

    def pair(pp, carries):
        j = 2 * pp
        produce(j + 1, s1_ref)
        carries = consume(j, s0_ref, carries, None)
        produce(j + 2, s0_ref)
        return consume(j + 1, s1_ref, carries, None)

    init = tuple((jnp.full((1, tq), NEG, F32), jnp.zeros((VT_ROWS, tq), F32)) for _ in range(hpb))
    produce(0, s0_ref)
    carries = lax.fori_loop(0, qi, pair, init)
    key = lax.broadcasted_iota(jnp.int32, (tk, tq), 0)
    qry = lax.broadcasted_iota(jnp.int32, (tk, tq), 1)
    produce(2 * qi + 1, s1_ref)
    carries = consume(2 * qi, s0_ref, carries, key <= qry)
    carries = consume(2 * qi + 1, s1_ref, carries, key + tk <= qry)
    for hh in range(hpb):
        acc = carries[hh][1]
        o_ref[0, hh * HEAD_DIM:(hh + 1) * HEAD_DIM, :] = acc[:HEAD_DIM] / acc[HEAD_DIM:HEAD_DIM + 1]


def _fox(ka, qta, vt, *, tq, tk, hpb):
    b, heads, l, _ = ka.shape
    ka5 = ka.reshape(b, heads, l // tk, tk, 2 * HEAD_DIM)
    return pl.pallas_call(
        functools.partial(_fox_kernel, tq=tq, tk=tk, hpb=hpb),
        grid=(b, heads // hpb, l // tq),
        in_specs=[pl.BlockSpec((1, hpb, l // tk, tk, 2 * HEAD_DIM), lambda bi, h, i: (bi, h, 0, 0, 0),
                               pipeline_mode=pl.Buffered(1)),
                  pl.BlockSpec((1, hpb, 2 * HEAD_DIM, tq), lambda bi, h, i: (bi, h, 0, i)),
                  pl.BlockSpec((1, hpb, l // tk, VT_ROWS, tk), lambda bi, h, i: (bi, h, 0, 0, 0),
                               pipeline_mode=pl.Buffered(1))],
        out_specs=pl.BlockSpec((1, hpb * HEAD_DIM, tq), lambda bi, h, i: (bi, h, i)),
        out_shape=jax.ShapeDtypeStruct((b, heads * HEAD_DIM, l), F32),
        scratch_shapes=[pltpu.VMEM((hpb, tk, tq), F32), pltpu.VMEM((hpb, tk, tq), F32)],
        compiler_params=_params(("arbitrary", "arbitrary", "arbitrary")),
        name="fox_prompt",
    )(ka5, qta, vt)


def _pastbias_kernel(pt_ref, lf_hbm, umat_ref, later_ref, o_ref, xbuf, sems, *, npages, heads):
    b = pl.program_id(0)
    nb = pl.num_programs(0)
    slot = lax.rem(b, 2)
    rows = npages * heads

    def copy(bb, sl, p):
        return pltpu.make_async_copy(lf_hbm.at[pt_ref[bb, p]], xbuf.at[sl, pl.ds(p * heads, heads)],
                                     sems.at[sl, p])

    def start_all(bb, sl):
        for p in range(npages):
            copy(bb, sl, p).start()

    @pl.when(b == 0)
    def _():
        start_all(0, 0)

    @pl.when(b + 1 < nb)
    def _():
        start_all(b + 1, 1 - slot)

    for p in range(npages):
        copy(b, slot, p).wait()

    x = xbuf[slot]
    hi, mid, lo = _split3(x)
    both = _sum3(_dot(jnp.concatenate([hi, mid, lo], axis=0), umat_ref[...]), rows, 0)
    rin = both[:, :LANES]
    tot = both[:, LANES:]
    hi, mid, lo = _split3(tot)
    off = _sum3(_dot(later_ref[...], jnp.concatenate([hi, mid, lo], axis=1)), LANES, 1)
    o_ref[0] = rin + off


def _pastbias(page_table, lf_t):
    n_pool, heads, page = lf_t.shape
    db, npages = page_table.shape
    assert page == LANES
    rows = npages * heads
    s = np.arange(page)
    umat = np.concatenate([s[:, None] > s[None, :], np.ones((page, page), bool)], axis=1)
    r = np.arange(rows)
    later = (r[None, :] // heads > r[:, None] // heads) & (r[None, :] % heads == r[:, None] % heads)
    umat, later = jnp.asarray(umat, BF16), jnp.asarray(later, BF16)
    grid_spec = pltpu.PrefetchScalarGridSpec(
        num_scalar_prefetch=1, grid=(db,),
        in_specs=[pl.BlockSpec(memory_space=pl.ANY),
                  pl.BlockSpec(umat.shape, lambda b, pt: (0, 0)),
                  pl.BlockSpec(later.shape, lambda b, pt: (0, 0))],
        out_specs=pl.BlockSpec((1, rows, page), lambda b, pt: (b, 0, 0)),
        scratch_shapes=[pltpu.VMEM((2, rows, page), F32), pltpu.SemaphoreType.DMA((2, npages))])
    return pl.pallas_call(
        functools.partial(_pastbias_kernel, npages=npages, heads=heads),
        grid_spec=grid_spec,
        out_shape=jax.ShapeDtypeStruct((db, rows, page), F32),
        compiler_params=_params(("arbitrary",)),
        name="pastbias",
    )(page_table, lf_t, umat, later)


def _decode_kernel(pt_ref, q_ref, kn_ref, vn_ref, cnt_ref, bias_ref, ck_hbm, cv_hbm, o_ref,
                   kbuf, vbuf, sems, qbd_ref, m_ref, l_ref, acc_ref, *, pps, heads, tnew):
    b = pl.program_id(0)
    s = pl.program_id(1)
    nb = pl.num_programs(0)
    ns = pl.num_programs(1)
    step = b * ns + s
    slot = lax.rem(step, 2)
    rows = heads * tnew
    width = heads * HEAD_DIM
    page = kbuf.shape[-1]
    rhead = lax.broadcasted_iota(jnp.int32, (rows, width), 0) // tnew
    chead = lax.broadcasted_iota(jnp.int32, (rows, width), 1) // HEAD_DIM

    def page_copy(bb, ss, sl, i, kv):
        src, dst = (ck_hbm, kbuf) if kv == 0 else (cv_hbm, vbuf)
        return pltpu.make_async_copy(src.at[pt_ref[bb, ss * pps + i]], dst.at[sl, i], sems.at[sl, kv, i])

    def start_step(bb, ss, sl):
        for i in range(pps):
            for kv in range(2):
                page_copy(bb, ss, sl, i, kv).start()

    @pl.when(step == 0)
    def _():
        start_step(0, 0, 0)

    @pl.when(step + 1 < nb * ns)
    def _():
        wrap = s + 1 == ns
        start_step(jnp.where(wrap, b + 1, b), jnp.where(wrap, 0, s + 1), 1 - slot)

    @pl.when(s == 0)
    def _():
        qtile = jnp.concatenate([q_ref[...]] * heads, axis=0)
        qbd_ref[...] = jnp.where(rhead == chead, qtile, 0.0).astype(BF16)
        m_ref[...] = jnp.full_like(m_ref, NEG)
        l_ref[...] = jnp.zeros_like(l_ref)
        acc_ref[...] = jnp.zeros_like(acc_ref)

    qbd = qbd_ref[...]

    def wait_pages(kv):
        for i in range(pps):
            page_copy(b, s, slot, i, kv).wait()

    def cat_pages(buf, kv, g):
        return jnp.concatenate([buf[slot, 2 * g + e].reshape(width, page).astype(BF16) for e in range(2)], axis=1)

    wait_pages(0)
    scores = []
    for g in range(pps // 2):
        bias = jnp.concatenate(
            [jnp.concatenate([jnp.broadcast_to(bias_ref[0, pl.ds((2 * g + e) * heads + h, 1), :], (tnew, page))
                              for h in range(heads)], axis=0) for e in range(2)], axis=1)
        scores.append(_dot(qbd, cat_pages(kbuf, 0, g)) + bias)
    m_old = m_ref[...]
    m_new = m_old
    for st in scores:
        m_new = jnp.maximum(m_new, jnp.max(st, axis=1, keepdims=True))
    alpha = jnp.exp(m_old - m_new)
    l = alpha * l_ref[...]
    acc = alpha * acc_ref[...]
    wait_pages(1)
    for g in range(pps // 2):
        p = jnp.exp(scores[g] - m_new)
        l = l + jnp.sum(p, axis=1, keepdims=True)
        acc = acc + _dot_nt(p.astype(BF16), cat_pages(vbuf, 1, g))
    m_ref[...] = m_new
    l_ref[...] = l
    acc_ref[...] = acc

    @pl.when(s == ns - 1)
    def _():
        st = _dot_nt(qbd, kn_ref[...].astype(BF16))
        bias = jnp.concatenate(
            [jnp.broadcast_to(cnt_ref[0, h:h + 1, :], (tnew, tnew)) for h in range(heads)], axis=0)
        qi = lax.broadcasted_iota(jnp.int32, (rows, tnew), 0) % tnew
        kj = lax.broadcasted_iota(jnp.int32, (rows, tnew), 1)
        st = jnp.where(kj <= qi, st - bias, NEG)
        m_fin = jnp.maximum(m_new, jnp.max(st, axis=1, keepdims=True))
        a2 = jnp.exp(m_new - m_fin)
        p = jnp.exp(st - m_fin)
        lf = a2 * l + jnp.sum(p, axis=1, keepdims=True)
        accf = a2 * acc + _dot(p.astype(BF16), vn_ref[...].astype(BF16))
        res = jnp.where(rhead == chead, accf / lf, 0.0)
        out = res[0:tnew]
        for h in range(1, heads):
            out = out + res[h * tnew:(h + 1) * tnew]
        o_ref[...] = out


def _decode(page_table, q, kn, vn, cnt, bias, ck_t, cv_t, *, tnew, pps):
    n_pool, heads, hd, page = ck_t.shape
    db, npages = page_table.shape
    width = heads * hd
    rows = heads * tnew
    assert pps % 2 == 0 and npages % pps == 0
    tokspec = pl.BlockSpec((tnew, width), lambda b, s, pt: (b, 0))
    pages = pltpu.VMEM((2, pps, heads, hd, page), F32)
    grid_spec = pltpu.PrefetchScalarGridSpec(
        num_scalar_prefetch=1, grid=(db, npages // pps),
        in_specs=[tokspec, tokspec, tokspec,
                  pl.BlockSpec((1, heads, tnew), lambda b, s, pt: (b, 0, 0)),
                  pl.BlockSpec((1, pps * heads, page), lambda b, s, pt: (b, s, 0)),
                  pl.BlockSpec(memory_space=pl.ANY), pl.BlockSpec(memory_space=pl.ANY)],
        out_specs=tokspec,
        scratch_shapes=[pages, pages, pltpu.SemaphoreType.DMA((2, 2, pps)),
                        pltpu.VMEM((rows, width), BF16), pltpu.VMEM((rows, 1), F32),
                        pltpu.VMEM((rows, 1), F32), pltpu.VMEM((rows, width), F32)])
    return pl.pallas_call(
        functools.partial(_decode_kernel, pps=pps, heads=heads, tnew=tnew),
        grid_spec=grid_spec,
        out_shape=jax.ShapeDtypeStruct((db * tnew, width), F32),
        compiler_params=_params(("arbitrary", "arbitrary")),
        name="decode_sample",
    )(page_table, q, kn, vn, cnt, bias, ck_t, cv_t)


def _s5_kernel(u_ref, b_ref, c_ref, lpr_ref, lpi_ref, d_ref, *rest, has_init, n, half):
    if has_init:
        h0r_ref, h0i_ref, y_ref, hr_ref, hi_ref, hlr, hli = rest
    else:
        y_ref, hr_ref, hi_ref, hlr, hli, hpr, hpi, car, cai = rest
    nslab = u_ref.shape[0]
    hs = nslab // 2

    def u_rows(j):
        return [u_ref[s, pl.ds(j, n, stride=CHUNK), :] for s in range(nslab)]

    lr1, li1 = lpr_ref[0:1, :], lpi_ref[0:1, :]
    hr = hi = None
    for j in range(CHUNK):
        us = u_rows(j)
        ua = jnp.concatenate(us[:hs], axis=1).astype(BF16)
        ub = jnp.concatenate(us[hs:], axis=1).astype(BF16)
        ba = _dot(ua, b_ref[0])
        bb = _dot(ub, b_ref[1])
        bur = jnp.concatenate([ba[:, :half], bb[:, :half]], axis=1)
        bui = jnp.concatenate([ba[:, half:], bb[:, half:]], axis=1)
        if hr is None:
            hr, hi = bur, bui
        else:
            hr, hi = lr1 * hr - li1 * hi + bur, lr1 * hi + li1 * hr + bui
        hlr[j] = hr
        hli[j] = hi

    if has_init:
        pr_all, pi_all = h0r_ref[...], h0i_ref[...]
    else:
        i = pl.program_id(1)

        @pl.when(i == 0)
        def _():
            car[...] = jnp.zeros_like(car)
            cai[...] = jnp.zeros_like(cai)

        ltr, lti = lpr_ref[CHUNK - 1:CHUNK, :], lpi_ref[CHUNK - 1:CHUNK, :]

        def step(c, carry):
            pr, pi_ = carry
            hpr[pl.ds(c, 1), :] = pr
            hpi[pl.ds(c, 1), :] = pi_
            er = hlr[CHUNK - 1, pl.ds(c, 1), :]
            ei = hli[CHUNK - 1, pl.ds(c, 1), :]
            return ltr * pr - lti * pi_ + er, ltr * pi_ + lti * pr + ei

        pr, pi_ = lax.fori_loop(0, n, step, (car[...], cai[...]))
        car[...] = pr
        cai[...] = pi_
        hr_ref[0] = pr
        hi_ref[0] = pi_
        pr_all, pi_all = hpr[...], hpi[...]

    for j in range(CHUNK):
        lr, li = lpr_ref[j:j + 1, :], lpi_ref[j:j + 1, :]
        fr = hlr[j] + lr * pr_all - li * pi_all
        fi = hli[j] + lr * pi_all + li * pr_all
        if has_init and j == CHUNK - 1:
            hr_ref[...] = fr
            hi_ref[...] = fi
        la = jnp.concatenate([fr[:, :half], fi[:, :half]], axis=1).astype(BF16)
        lb = jnp.concatenate([fr[:, half:], fi[:, half:]], axis=1).astype(BF16)
        y = jnp.concatenate([_dot(la, c_ref[0]), _dot(lb, c_ref[1])], axis=1)
        us = u_rows(j)
        for s in range(nslab):
            y_ref[s, pl.ds(j, n, stride=CHUNK), :] = (
                y[:, s * LANES:(s + 1) * LANES] + d_ref[:, s * LANES:(s + 1) * LANES] * us[s])


def _s5(u_slabs, bmat, cmat, lpr, lpi, dskip, h0=None, *, nseq, tm):
    nslab, t, _ = u_slabs.shape
    l = t // nseq
    n = tm // CHUNK
    sw = lpr.shape[1]
    half = sw // 2
    has_init = h0 is not None
    consts = (bmat, cmat, lpr, lpi, dskip)
    scratch = [pltpu.VMEM((CHUNK, n, sw), F32), pltpu.VMEM((CHUNK, n, sw), F32)]
    if has_init:
        nb = t // tm
        grid = (1, nb)
        slab_spec = pl.BlockSpec((nslab, tm, LANES), lambda bi, i: (0, i, 0))
        st_spec = pl.BlockSpec((n, sw), lambda bi, i: (i, 0))
        ins = [u_slabs, *consts, h0[0], h0[1]]
        in_specs = [slab_spec] + [_const_spec(c.shape) for c in consts] + [st_spec, st_spec]
        st_shape = jax.ShapeDtypeStruct((t // CHUNK, sw), F32)
    else:
        nb = l // tm
        grid = (nseq, nb)
        slab_spec = pl.BlockSpec((nslab, tm, LANES), lambda bi, i: (0, bi * nb + i, 0))
        st_spec = pl.BlockSpec((1, 1, sw), lambda bi, i: (bi, 0, 0))
        ins = [u_slabs, *consts]
        in_specs = [slab_spec] + [_const_spec(c.shape) for c in consts]
        st_shape = jax.ShapeDtypeStruct((nseq, 1, sw), F32)
        scratch += [pltpu.VMEM((n, sw), F32), pltpu.VMEM((n, sw), F32),
                    pltpu.VMEM((1, sw), F32), pltpu.VMEM((1, sw), F32)]
    return pl.pallas_call(
        functools.partial(_s5_kernel, has_init=has_init, n=n, half=half),
        grid=grid, in_specs=in_specs,
        out_specs=[slab_spec, st_spec, st_spec],
        out_shape=[jax.ShapeDtypeStruct((nslab, t, LANES), F32), st_shape, st_shape],
        scratch_shapes=scratch,
        compiler_params=_params(("arbitrary", "arbitrary")),
        name="s5_sample" if has_init else "s5_prompt",
    )(*ins)


def _gelu_tanh(x):
    return x * (0.5 * (1.0 + jnp.tanh(math.sqrt(2.0 / math.pi) * (x + 0.044715 * (x * x * x)))))


def _tail_kernel(x_ref, att_ref, y_ref, wglu_ref, gatt_ref, gssm_ref, woa_ref, wos_ref, g2_ref,
                 wg_ref, wu_ref, wd_ref, gf_ref, o_ref, *, att_transposed, sw):
    x = x_ref[...]
    yv = jnp.concatenate([y_ref[s] for s in range(y_ref.shape[0])], axis=1)
    gv = _dot(_gelu_tanh(yv).astype(BF16), wglu_ref[...])
    ssm = gv[:, :sw] * jax.nn.sigmoid(gv[:, sw:])
    ssm_n = _rms_rows(ssm, gssm_ref[...]).astype(BF16)
    if att_transposed:
        at = att_ref[0]
        r = lax.rsqrt(jnp.mean(at * at, axis=0, keepdims=True) + EPS)
        att_n = (at * r * gatt_ref[...]).astype(BF16)
        ho = _dot_tn(att_n, woa_ref[...])
    else:
        ho = _dot(_rms_rows(att_ref[...], gatt_ref[...]).astype(BF16), woa_ref[...])
    h = x + ho + _dot(ssm_n, wos_ref[...])
    hn = _rms_rows(h, g2_ref[...]).astype(BF16)
    gt = _dot(hn, wg_ref[...])
    act = (gt * jax.nn.sigmoid(gt) * _dot(hn, wu_ref[...])).astype(BF16)
    y = h + _dot(act, wd_ref[...])
    o_ref[...] = _rms_rows(y, gf_ref[...])


def _tail(x, att, y_slabs, wglu, gatt, gssm, woa, wos, g2, wg, wu, wd, gf, *, att_transposed, tm):
    b, l, d = x.shape
    t = b * l
    nb = l // tm
    nslab = y_slabs.shape[0]
    sw = nslab * LANES
    tok = lambda bi, i: (bi * nb + i, 0)
    if att_transposed:
        att_spec = pl.BlockSpec((1, att.shape[1], tm), lambda bi, i: (bi, 0, i))
    else:
        att_spec = pl.BlockSpec((tm, att.shape[1]), tok)
    consts_a = (wglu, gatt, gssm, woa, wos, g2, wg, wu, wd, gf)
    out = pl.pallas_call(
        functools.partial(_tail_kernel, att_transposed=att_transposed, sw=sw),
        grid=(b, nb),
        in_specs=[pl.BlockSpec((tm, d), tok), att_spec,
                  pl.BlockSpec((nslab, tm, LANES), lambda bi, i: (0, bi * nb + i, 0))]
                 + [_const_spec(c.shape) for c in consts_a],
        out_specs=pl.BlockSpec((tm, d), tok),
        out_shape=jax.ShapeDtypeStruct((t, d), F32),
        compiler_params=_params(("arbitrary", "arbitrary")),
        name="tail_prompt" if att_transposed else "tail_sample",
    )(x.reshape(t, d), att, y_slabs, *consts_a)
    return out.reshape(b, l, d)


def _blockdiag(m):
    g, r, c = m.shape
    eye = jnp.eye(g, dtype=m.dtype)
    return (m[:, :, None, :] * eye[:, None, :, None]).reshape(g * r, g * c)


def _s5_params(a_re, a_im, log_dt, b_re, b_im, c_re, c_im, d):
    ar, ai = a_re.astype(F32), a_im.astype(F32)
    g, p = ar.shape
    dt = jnp.exp(log_dt.astype(F32))[:, None]
    mag = jnp.exp(ar * dt)
    lam_re = mag * jnp.cos(ai * dt)
    lam_im = mag * jnp.sin(ai * dt)
    den = ar * ar + ai * ai
    q_re = ((lam_re - 1.0) * ar + lam_im * ai) / den
    q_im = (lam_im * ar - (lam_re - 1.0) * ai) / den
    br, bi = b_re.astype(F32), b_im.astype(F32)
    bb_re = (q_re[..., None] * br - q_im[..., None] * bi).transpose(0, 2, 1)
    bb_im = (q_re[..., None] * bi + q_im[..., None] * br).transpose(0, 2, 1)
    gh = g // 2
    bmat = jnp.stack([jnp.concatenate([_blockdiag(bb_re[s]), _blockdiag(bb_im[s])], axis=1)
                      for s in (slice(0, gh), slice(gh, g))]).astype(BF16)
    cr = c_re.astype(F32).transpose(0, 2, 1)
    ci = c_im.astype(F32).transpose(0, 2, 1)
    cmat = jnp.stack([jnp.concatenate([_blockdiag(cr[s]), -_blockdiag(ci[s])], axis=0)
                      for s in (slice(0, gh), slice(gh, g))]).astype(BF16)
    pr, pi_ = [lam_re], [lam_im]
    for _ in range(CHUNK - 1):
        pr, pi_ = (pr + [pr[-1] * lam_re - pi_[-1] * lam_im], pi_ + [pr[-1] * lam_im + pi_[-1] * lam_re])
    lpr = jnp.stack(pr).reshape(CHUNK, g * p)
    lpi = jnp.stack(pi_).reshape(CHUNK, g * p)
    return bmat, cmat, lpr, lpi, d.astype(F32).reshape(1, -1)


def _proj_params(w_in, b_f, norm1_g, d):
    att = d // 2
    heads = att // HEAD_DIM
    scale = HEAD_DIM ** -0.5
    w = w_in.astype(F32)
    wq = w[:, :att] * scale
    wk, wv = w[:, att:2 * att], w[:, 2 * att:3 * att]
    wfl = w[:, 3 * att:3 * att + heads]
    wu = w[:, 3 * att + heads:]
    wf = jnp.zeros((d, LANES), F32).at[:, :3 * heads].set(jnp.tile(wfl, (1, 3))).astype(BF16)
    bf = jnp.zeros((1, LANES), F32).at[0, :3 * heads].set(jnp.tile(b_f.astype(F32), 3))
    wft = jnp.zeros((2 * heads, d), F32).at[:heads].set(wfl.T).astype(BF16)
    bft = jnp.zeros((2 * heads, 1), F32).at[:heads, 0].set(b_f.astype(F32))
    pk = np.zeros((LANES, att), np.float32)
    pq = np.zeros((att, 4 * heads), np.float32)
    for h in range(heads):
        base = h * HEAD_DIM
        for j in range(3):
            pk[3 * heads, base + j] = 1.0
            pk[j * heads + h, base + 3 + j] = 1.0
            pq[base + j, j * heads + h] = 1.0
            pq[base + 3 + j, 3 * heads] = -1.0
    return dict(
        g1=norm1_g.astype(F32).reshape(1, d),
        wn_prompt=jnp.concatenate([wk, wu], axis=1).astype(BF16),
        wn_sample=jnp.concatenate([wq, wk, wv, wu], axis=1).astype(BF16),
        wf=wf, bf=bf, wt=jnp.concatenate([wq * LOG2E, wk, wv], axis=1).T.astype(BF16), wft=wft, bft=bft,
        pk=jnp.asarray(pk, BF16), pq=jnp.asarray(pq, BF16))


def _layer(xp, xs, cache_k, cache_v, cache_logf, st_re, st_im, page_table, w):
    b, l, d = xp.shape
    db, tnew, _ = xs.shape
    att = d // 2
    heads = att // HEAD_DIM
    sw = d - att
    g = sw // SSM_GROUP
    tm, tq, tk, hpb = 512, 512, 256, 4
    tm_s5 = min(1024, l)
    pps = min(16, page_table.shape[1])

    pp = _proj_params(w["w_in"], w["b_f"], w["norm1_g"], d)
    bmat, cmat, lpr, lpi, dskip = _s5_params(w["a_re"], w["a_im"], w["log_dt"], w["b_re"], w["b_im"],
                                             w["c_re"], w["c_im"], w["d"])
    tail_w = (w["w_glu"].astype(BF16), None, w["norm_ssm_g"].astype(F32).reshape(1, sw),
              w["w_out"][:att].astype(BF16), w["w_out"][att:].astype(BF16),
              w["norm2_g"].astype(F32).reshape(1, d), w["w_gate"].astype(BF16), w["w_up"].astype(BF16),
              w["w_down"].astype(BF16), w["norm_f_g"].astype(F32).reshape(1, d))
    gatt = w["norm_attn_g"].astype(F32)
    pconst = (pp["g1"], None, pp["wf"], pp["bf"], pp["wt"], pp["wft"], pp["bft"], pp["pk"], pp["pq"])

    def consts(kind):
        c = list(pconst)
        c[1] = pp["wn_" + kind]
        return c

    up, kt, vt32, lft, ka, qta, vt = _proj(xp, *consts("prompt"), prompt=True, tm=tm, tk=tk)
    att_t = _fox(ka, qta, vt, tq=tq, tk=tk, hpb=hpb)
    yp_s, hrp, hip = _s5(up, bmat, cmat, lpr, lpi, dskip, nseq=b, tm=tm_s5)
    tw = list(tail_w)
    tw[1] = gatt.reshape(att, 1)
    y_prompt = _tail(xp, att_t, yp_s, *tw, att_transposed=True, tm=tm)

    ts = db * tnew
    tms = min(tm, ts)
    us, qs, ks, vs, lfs, cnt = _proj(xs.reshape(1, ts, d), *consts("sample"), prompt=False, tm=tms, tk=tk)
    cnt = cnt.reshape(heads, db, tnew).transpose(1, 0, 2)
    bias = _pastbias(page_table, cache_logf.transpose(0, 2, 1))
    att_s = _decode(page_table, qs, ks, vs, cnt, bias, cache_k.transpose(0, 2, 3, 1),
                    cache_v.transpose(0, 2, 3, 1), tnew=tnew, pps=pps)
    h0 = (st_re.astype(F32).reshape(db, g * STATE_DIM), st_im.astype(F32).reshape(db, g * STATE_DIM))
    ys_s, hrs, his = _s5(us, bmat, cmat, lpr, lpi, dskip, h0, nseq=db, tm=tms)
    tw[1] = gatt.reshape(1, att)
    y_sample = _tail(xs.reshape(1, ts, d), att_s, ys_s, *tw, att_transposed=False, tm=tms)

    return (y_prompt, y_sample.reshape(db, tnew, d),
            kt.transpose(0, 3, 1, 2)[None], vt32.transpose(0, 3, 1, 2)[None],
            lft.transpose(0, 2, 1)[None],
            hrp.reshape(1, b, g, STATE_DIM), hip.reshape(1, b, g, STATE_DIM),
            ks.reshape(1, db, tnew, heads, HEAD_DIM), vs.reshape(1, db, tnew, heads, HEAD_DIM),
            lfs.reshape(1, db, tnew, heads),
            hrs.reshape(1, db, g, STATE_DIM), his.reshape(1, db, g, STATE_DIM))


def kernel(x_prompt, x_sample, cache_k, cache_v, cache_logf, state_ssm_re, state_ssm_im, page_table, norm1_g, w_in, b_f, ssm_a_re, ssm_a_im, ssm_log_dt, ssm_b_re, ssm_b_im, ssm_c_re, ssm_c_im, ssm_d, w_glu, norm_attn_g, norm_ssm_g, w_out, norm2_g, w_gate, w_up, w_down, norm_f_g):
    assert w_in.shape[0] == 1, "single-layer trunk"
    w = dict(norm1_g=norm1_g[0], w_in=w_in[0], b_f=b_f[0], a_re=ssm_a_re[0], a_im=ssm_a_im[0],
             log_dt=ssm_log_dt[0], b_re=ssm_b_re[0], b_im=ssm_b_im[0], c_re=ssm_c_re[0], c_im=ssm_c_im[0],
             d=ssm_d[0], w_glu=w_glu[0], norm_attn_g=norm_attn_g[0], norm_ssm_g=norm_ssm_g[0],
             w_out=w_out[0], norm2_g=norm2_g[0], w_gate=w_gate[0], w_up=w_up[0], w_down=w_down[0],
             norm_f_g=norm_f_g)
    return _layer(x_prompt, x_sample, cache_k[0], cache_v[0], cache_logf[0], state_ssm_re[0],
                  state_ssm_im[0], page_table, w)
```

```python
import functools
import math

import jax
import jax.numpy as jnp
import numpy as np
from jax import lax
from jax.experimental import pallas as pl
from jax.experimental.pallas import tpu as pltpu

F32 = jnp.float32
BF16 = jnp.bfloat16

HEAD_DIM = 64
SSM_GROUP = 16
STATE_DIM = 64
EPS = 1e-6
NEG = -1e30
LANES = 128
CHUNK = 8
VT_ROWS = HEAD_DIM + 16
LOG2E = math.log2(math.e)
VMEM_LIMIT = 56 * 1024 * 1024


def _dot(a, b):
    return jnp.dot(a, b, preferred_element_type=F32)


def _dot_nt(a, b):
    return lax.dot_general(a, b, (((1,), (1,)), ((), ())), preferred_element_type=F32)


def _dot_tn(a, b):
    return lax.dot_general(a, b, (((0,), (0,)), ((), ())), preferred_element_type=F32)


def _split3(x):
    hi = x.astype(BF16)
    r1 = x - hi.astype(F32)
    mid = r1.astype(BF16)
    lo = (r1 - mid.astype(F32)).astype(BF16)
    return hi, mid, lo


def _sum3(x, n, axis):
    if axis == 0:
        return x[:n] + x[n:2 * n] + x[2 * n:3 * n]
    return x[:, :n] + x[:, n:2 * n] + x[:, 2 * n:3 * n]


def _rms_rows(x, g):
    r = lax.rsqrt(jnp.mean(x * x, axis=-1, keepdims=True) + EPS)
    return x * r * g


def _log_sigmoid(x):
    return jnp.minimum(x, 0.0) - jnp.log1p(jnp.exp(-jnp.abs(x)))


def _params(sem, vmem=VMEM_LIMIT):
    return pltpu.CompilerParams(dimension_semantics=sem, vmem_limit_bytes=vmem)


def _const_spec(shape):
    nd = len(shape)
    return pl.BlockSpec(shape, lambda *_: (0,) * nd, pipeline_mode=pl.Buffered(1))


def _proj_kernel(x_ref, g_ref, wn_ref, wf_ref, bf_ref, wt_ref, wft_ref, bft_ref, pk_ref, pq_ref,
                 *out_and_scratch, prompt, tm, att, heads, tk):
    if prompt:
        (u_ref, kt_ref, vt32_ref, lft_ref, ka_ref, qta_ref, vt_ref, car_ref, cart_ref) = out_and_scratch
    else:
        (u_ref, q_ref, k_ref, v_ref, lf_ref, cnt_ref) = out_and_scratch
    i = pl.program_id(1)
    xn = _rms_rows(x_ref[...], g_ref[...]).astype(BF16)
    z = _dot(xn, wn_ref[...])
    if prompt:
        k, u = z[:, :att], z[:, att:]
    else:
        q, k, v, u = z[:, :att], z[:, att:2 * att], z[:, 2 * att:3 * att], z[:, 3 * att:]
        q_ref[...] = q
        k_ref[...] = k
        v_ref[...] = v
    for s in range(u.shape[1] // LANES):
        u_ref[s] = u[:, s * LANES:(s + 1) * LANES]
    lft = _log_sigmoid(_dot_nt(wft_ref[...], xn) + bft_ref[...])
    row = lax.broadcasted_iota(jnp.int32, (tm, tm), 0)
    col = lax.broadcasted_iota(jnp.int32, (tm, tm), 1)
    if prompt:
        triu = jnp.where(row <= col, 1.0, 0.0).astype(BF16)
    else:
        same = (row // CHUNK) == (col // CHUNK)
        triu = jnp.where(same & (row <= col), 1.0, 0.0).astype(BF16)
    hi, mid, lo = _split3(lft[:heads])
    zpad = jnp.zeros((heads, tm), BF16)
    ct = _sum3(_dot(jnp.concatenate([hi, mid, lo, zpad], axis=0), triu), heads, 0)
    if not prompt:
        lf_ref[...] = _log_sigmoid(_dot(xn, wf_ref[...]) + bf_ref[...])[:, :heads]
        cnt_ref[...] = ct
        return

    @pl.when(i == 0)
    def _():
        car_ref[...] = jnp.zeros_like(car_ref)
        cart_ref[...] = jnp.zeros_like(cart_ref)

    lft_ref[0] = lft[:heads]
    ct = ct + cart_ref[:, 0:1]
    cart_ref[...] = jnp.broadcast_to(ct[:, tm - 1:tm], cart_ref.shape)
    lf = _log_sigmoid(_dot(xn, wf_ref[...]) + bf_ref[...])
    tril = jnp.where(col <= row, 1.0, 0.0).astype(BF16)
    hi, mid, lo = _split3(lf)
    c = _sum3(_dot(tril, jnp.concatenate([hi, mid, lo], axis=1)), LANES, 1) + car_ref[...]
    car_ref[...] = c[tm - 1:tm, :]
    lane = lax.broadcasted_iota(jnp.int32, (tm, LANES), 1)
    chi, cmid, clo = _split3(c * LOG2E)
    cpack = jnp.where(lane < heads, chi,
                      jnp.where(lane < 2 * heads, cmid,
                                jnp.where(lane < 3 * heads, clo,
                                          jnp.where(lane == 3 * heads, 1.0, 0.0).astype(BF16))))
    kaug = _dot(cpack, pk_ref[...])
    zt = _dot_nt(wt_ref[...], xn)
    srow = lax.broadcasted_iota(jnp.int32, (4 * heads, tm), 0)
    thi, tmid, tlo = _split3(ct * LOG2E)
    ctpack = jnp.concatenate([thi, tmid, tlo, jnp.ones((heads, tm), BF16)], axis=0)
    ctpack = jnp.where(srow <= 3 * heads, ctpack, jnp.zeros_like(ctpack))
    qaug = _dot(pq_ref[...], ctpack)
    for h in range(heads):
        sl = slice(h * HEAD_DIM, (h + 1) * HEAD_DIM)
        kt = zt[att + h * HEAD_DIM:att + (h + 1) * HEAD_DIM, :]
        vt = zt[2 * att + h * HEAD_DIM:2 * att + (h + 1) * HEAD_DIM, :]
        kt_ref[0, h] = kt
        vt32_ref[0, h] = vt
        ka_ref[0, h, :, 0:HEAD_DIM] = k[:, sl].astype(BF16)
        ka_ref[0, h, :, HEAD_DIM:2 * HEAD_DIM] = kaug[:, sl].astype(BF16)
        qta_ref[0, h, 0:HEAD_DIM, :] = zt[sl, :].astype(BF16)
        qta_ref[0, h, HEAD_DIM:2 * HEAD_DIM, :] = qaug[sl, :].astype(BF16)
        for jj in range(tm // tk):
            vt_ref[0, h, jj, 0:HEAD_DIM, :] = vt[:, jj * tk:(jj + 1) * tk].astype(BF16)
            vt_ref[0, h, jj, HEAD_DIM:VT_ROWS, :] = jnp.ones((VT_ROWS - HEAD_DIM, tk), BF16)


def _proj(x, g1, wn, wf, bf, wt, wft, bft, pk, pq, *, prompt, tm, tk):
    b, l, d = x.shape
    att = d // 2
    heads = att // HEAD_DIM
    nslab = (d - att) // LANES
    t = b * l
    nb = l // tm
    x2 = x.reshape(t, d)
    tok = lambda bi, i: (bi * nb + i, 0)
    out_shape = [jax.ShapeDtypeStruct((nslab, t, LANES), F32)]
    out_specs = [pl.BlockSpec((nslab, tm, LANES), lambda bi, i: (0, bi * nb + i, 0))]
    scratch = []
    if prompt:
        hd_t = pl.BlockSpec((1, heads, HEAD_DIM, tm), lambda bi, i: (bi, 0, 0, i))
        out_shape += [jax.ShapeDtypeStruct((b, heads, HEAD_DIM, l), F32),
                      jax.ShapeDtypeStruct((b, heads, HEAD_DIM, l), F32),
                      jax.ShapeDtypeStruct((b, heads, l), F32),
                      jax.ShapeDtypeStruct((b, heads, l, 2 * HEAD_DIM), BF16),
                      jax.ShapeDtypeStruct((b, heads, 2 * HEAD_DIM, l), BF16),
                      jax.ShapeDtypeStruct((b, heads, l // tk, VT_ROWS, tk), BF16)]
        out_specs += [hd_t, hd_t,
                      pl.BlockSpec((1, heads, tm), lambda bi, i: (bi, 0, i)),
                      pl.BlockSpec((1, heads, tm, 2 * HEAD_DIM), lambda bi, i: (bi, 0, i, 0)),
                      pl.BlockSpec((1, heads, 2 * HEAD_DIM, tm), lambda bi, i: (bi, 0, 0, i)),
                      pl.BlockSpec((1, heads, tm // tk, VT_ROWS, tk), lambda bi, i: (bi, 0, i, 0, 0))]
        scratch = [pltpu.VMEM((1, LANES), F32), pltpu.VMEM((heads, LANES), F32)]
    else:
        out_shape += [jax.ShapeDtypeStruct((t, att), F32)] * 3 + [
            jax.ShapeDtypeStruct((t, heads), F32), jax.ShapeDtypeStruct((heads, t), F32)]
        out_specs += [pl.BlockSpec((tm, att), tok)] * 3 + [
            pl.BlockSpec((tm, heads), tok), pl.BlockSpec((heads, tm), lambda bi, i: (0, bi * nb + i))]
    consts = (g1, wn, wf, bf, wt, wft, bft, pk, pq)
    return pl.pallas_call(
        functools.partial(_proj_kernel, prompt=prompt, tm=tm, att=att, heads=heads, tk=tk),
        grid=(b, nb),
        in_specs=[pl.BlockSpec((tm, d), tok)] + [_const_spec(c.shape) for c in consts],
        out_specs=out_specs, out_shape=out_shape, scratch_shapes=scratch,
        compiler_params=_params(("arbitrary", "arbitrary")),
        name="proj_prompt" if prompt else "proj_sample",
    )(x2, *consts)


def _fox_kernel(ka_ref, qta_ref, vt_ref, o_ref, s0_ref, s1_ref, *, tq, tk, hpb):
    assert tq == 2 * tk
    qi = pl.program_id(2)
    qts = [qta_ref[0, hh] for hh in range(hpb)]

    def produce(j, buf):
        for hh in range(hpb):
            buf[hh] = _dot(ka_ref[0, hh, j], qts[hh])

    def consume(j, buf, carries, masked):
        ps, ms, alphas = [], [], []
        for hh in range(hpb):
            s = buf[hh] if masked is None else jnp.where(masked, buf[hh], NEG)
            m = carries[hh][0]
            m_new = jnp.maximum(m, jnp.max(s, axis=0, keepdims=True))
            ps.append(jnp.exp2(s - m_new).astype(BF16))
            alphas.append(jnp.exp2(m - m_new))
            ms.append(m_new)
        return tuple((ms[hh], alphas[hh] * carries[hh][1] + _dot(vt_ref[0, hh, j], ps[hh]))
                     for hh in range(hpb))

    def pair(pp, carries):
        j = 2 * pp
        produce(j + 1, s1_ref)
        carries = consume(j, s0_ref, carries, None)
        produce(j + 2, s0_ref)
        return consume(j + 1, s1_ref, carries, None)

    init = tuple((jnp.full((1, tq), NEG, F32), jnp.zeros((VT_ROWS, tq), F32)) for _ in range(hpb))
    produce(0, s0_ref)
    carries = lax.fori_loop(0, qi, pair, init)
    key = lax.broadcasted_iota(jnp.int32, (tk, tq), 0)
    qry = lax.broadcasted_iota(jnp.int32, (tk, tq), 1)
    produce(2 * qi + 1, s1_ref)
    carries = consume(2 * qi, s0_ref, carries, key <= qry)
    carries = consume(2 * qi + 1, s1_ref, carries, key + tk <= qry)
    for hh in range(hpb):
        acc = carries[hh][1]
        o_ref[0, hh * HEAD_DIM:(hh + 1) * HEAD_DIM, :] = acc[:HEAD_DIM] / acc[HEAD_DIM:HEAD_DIM + 1]


def _fox(ka, qta, vt, *, tq, tk, hpb):
    b, heads, l, _ = ka.shape
    ka5 = ka.reshape(b, heads, l // tk, tk, 2 * HEAD_DIM)
    return pl.pallas_call(
        functools.partial(_fox_kernel, tq=tq, tk=tk, hpb=hpb),
        grid=(b, heads // hpb, l // tq),
        in_specs=[pl.BlockSpec((1, hpb, l // tk, tk, 2 * HEAD_DIM), lambda bi, h, i: (bi, h, 0, 0, 0),
                               pipeline_mode=pl.Buffered(1)),
                  pl.BlockSpec((1, hpb, 2 * HEAD_DIM, tq), lambda bi, h, i: (bi, h, 0, i)),
                  pl.BlockSpec((1, hpb, l // tk, VT_ROWS, tk), lambda bi, h, i: (bi, h, 0, 0, 0),
                               pipeline_mode=pl.Buffered(1))],
        out_specs=pl.BlockSpec((1, hpb * HEAD_DIM, tq), lambda bi, h, i: (bi, h, i)),
        out_shape=jax.ShapeDtypeStruct((b, heads * HEAD_DIM, l), F32),
        scratch_shapes=[pltpu.VMEM((hpb, tk, tq), F32), pltpu.VMEM((hpb, tk, tq), F32)],
        compiler_params=_params(("arbitrary", "arbitrary", "arbitrary")),
        name="fox_prompt",
    )(ka5, qta, vt)


def _past_bias(x, umat, later):
    rows = x.shape[0]
    hi, mid, lo = _split3(x)
    both = _sum3(_dot(jnp.concatenate([hi, mid, lo], axis=0), umat), rows, 0)
    rin = both[:, :LANES]
    tot = both[:, LANES:]
    hi, mid, lo = _split3(tot)
    return rin + _sum3(_dot(later, jnp.concatenate([hi, mid, lo], axis=1)), LANES, 1)


def _past_bias_consts(npages, heads, page):
    s = np.arange(page)
    umat = np.concatenate([s[:, None] > s[None, :], np.ones((page, page), bool)], axis=1)
    r = np.arange(npages * heads)
    later = (r[None, :] // heads > r[:, None] // heads) & (r[None, :] % heads == r[:, None] % heads)
    return jnp.asarray(umat, BF16), jnp.asarray(later, BF16)


def _decode_kernel(pt_ref, q_ref, kn_ref, vn_ref, cnt_ref, umat_ref, later_ref, lf_hbm, ck_hbm, cv_hbm, o_ref,
                   kbuf, vbuf, sems, xbuf, xsems, bias_ref, qbd_ref, m_ref, l_ref, acc_ref,
                   *, pps, npages, heads, tnew):
    b = pl.program_id(0)
    s = pl.program_id(1)
    nb = pl.num_programs(0)
    ns = pl.num_programs(1)
    step = b * ns + s
    slot = lax.rem(step, 2)
    bslot = lax.rem(b, 2)
    rows = heads * tnew
    width = heads * HEAD_DIM
    page = kbuf.shape[-1]
    rhead = lax.broadcasted_iota(jnp.int32, (rows, width), 0) // tnew
    chead = lax.broadcasted_iota(jnp.int32, (rows, width), 1) // HEAD_DIM

    def page_copy(bb, ss, sl, i, kv):
        src, dst = (ck_hbm, kbuf) if kv == 0 else (cv_hbm, vbuf)
        return pltpu.make_async_copy(src.at[pt_ref[bb, ss * pps + i]], dst.at[sl, i], sems.at[sl, kv, i])

    def start_step(bb, ss, sl):
        for i in range(pps):
            for kv in range(2):
                page_copy(bb, ss, sl, i, kv).start(priority=kv)

    def logf_copy(bb, sl, p):
        return pltpu.make_async_copy(lf_hbm.at[pt_ref[bb, p]], xbuf.at[sl, pl.ds(p * heads, heads)],
                                     xsems.at[sl, p])

    @pl.when(step == 0)
    def _():
        for p in range(npages):
            logf_copy(0, 0, p).start()
        start_step(0, 0, 0)

    @pl.when((s == 0) & (b + 1 < nb))
    def _():
        for p in range(npages):
            logf_copy(b + 1, 1 - bslot, p).start()

    @pl.when(step + 1 < nb * ns)
    def _():
        wrap = s + 1 == ns
        start_step(jnp.where(wrap, b + 1, b), jnp.where(wrap, 0, s + 1), 1 - slot)

    @pl.when(s == 0)
    def _():
        qtile = jnp.concatenate([q_ref[...]] * heads, axis=0)
        qbd_ref[...] = jnp.where(rhead == chead, qtile, 0.0).astype(BF16)
        m_ref[...] = jnp.full_like(m_ref, NEG)
        l_ref[...] = jnp.zeros_like(l_ref)
        acc_ref[...] = jnp.zeros_like(acc_ref)
        for p in range(npages):
            logf_copy(b, bslot, p).wait()
        bias_ref[...] = _past_bias(xbuf[bslot], umat_ref[...], later_ref[...])

    qbd = qbd_ref[...]

    def wait_pages(kv):
        for i in range(pps):
            page_copy(b, s, slot, i, kv).wait()

    def cat_pages(buf, kv, g):
        return jnp.concatenate([buf[slot, 2 * g + e].reshape(width, page).astype(BF16) for e in range(2)], axis=1)

    wait_pages(0)
    scores = []
    for g in range(pps // 2):
        bias = jnp.concatenate(
            [jnp.concatenate([jnp.broadcast_to(bias_ref[pl.ds((s * pps + 2 * g + e) * heads + h, 1), :],
                                               (tnew, page))
                              for h in range(heads)], axis=0) for e in range(2)], axis=1)
        scores.append(_dot(qbd, cat_pages(kbuf, 0, g)) + bias)
    m_old = m_ref[...]
    m_new = m_old
    for st in scores:
        m_new = jnp.maximum(m_new, jnp.max(st, axis=1, keepdims=True))
    alpha = jnp.exp(m_old - m_new)
    l = alpha * l_ref[...]
    acc = alpha * acc_ref[...]
    wait_pages(1)
    for g in range(pps // 2):
        p = jnp.exp(scores[g] - m_new)
        l = l + jnp.sum(p, axis=1, keepdims=True)
        acc = acc + _dot_nt(p.astype(BF16), cat_pages(vbuf, 1, g))
    m_ref[...] = m_new
    l_ref[...] = l
    acc_ref[...] = acc

    @pl.when(s == ns - 1)
    def _():
        st = _dot_nt(qbd, kn_ref[...].astype(BF16))
        bias = jnp.concatenate(
            [jnp.broadcast_to(cnt_ref[0, h:h + 1, :], (tnew, tnew)) for h in range(heads)], axis=0)
        qi = lax.broadcasted_iota(jnp.int32, (rows, tnew), 0) % tnew
        kj = lax.broadcasted_iota(jnp.int32, (rows, tnew), 1)
        st = jnp.where(kj <= qi, st - bias, NEG)
        m_fin = jnp.maximum(m_new, jnp.max(st, axis=1, keepdims=True))
        a2 = jnp.exp(m_new - m_fin)
        p = jnp.exp(st - m_fin)
        lf = a2 * l + jnp.sum(p, axis=1, keepdims=True)
        accf = a2 * acc + _dot(p.astype(BF16), vn_ref[...].astype(BF16))
        res = jnp.where(rhead == chead, accf / lf, 0.0)
        out = res[0:tnew]
        for h in range(1, heads):
            out = out + res[h * tnew:(h + 1) * tnew]
        o_ref[...] = out


def _decode(page_table, q, kn, vn, cnt, lf_t, ck_t, cv_t, *, tnew, pps):
    n_pool, heads, hd, page = ck_t.shape
    db, npages = page_table.shape
    width = heads * hd
    rows = heads * tnew
    assert pps % 2 == 0 and npages % pps == 0 and page == LANES
    umat, later = _past_bias_consts(npages, heads, page)
    tokspec = pl.BlockSpec((tnew, width), lambda b, s, pt: (b, 0))
    pages = pltpu.VMEM((2, pps, heads, hd, page), F32)
    grid_spec = pltpu.PrefetchScalarGridSpec(
        num_scalar_prefetch=1, grid=(db, npages // pps),
        in_specs=[tokspec, tokspec, tokspec,
                  pl.BlockSpec((1, heads, tnew), lambda b, s, pt: (b, 0, 0)),
                  pl.BlockSpec(umat.shape, lambda b, s, pt: (0, 0)),
                  pl.BlockSpec(later.shape, lambda b, s, pt: (0, 0)),
                  pl.BlockSpec(memory_space=pl.ANY), pl.BlockSpec(memory_space=pl.ANY),
                  pl.BlockSpec(memory_space=pl.ANY)],
        out_specs=tokspec,
        scratch_shapes=[pages, pages, pltpu.SemaphoreType.DMA((2, 2, pps)),
                        pltpu.VMEM((2, npages * heads, page), F32), pltpu.SemaphoreType.DMA((2, npages)),
                        pltpu.VMEM((npages * heads, page), F32),
                        pltpu.VMEM((rows, width), BF16), pltpu.VMEM((rows, 1), F32),
                        pltpu.VMEM((rows, 1), F32), pltpu.VMEM((rows, width), F32)])
    return pl.pallas_call(
        functools.partial(_decode_kernel, pps=pps, npages=npages, heads=heads, tnew=tnew),
        grid_spec=grid_spec,
        out_shape=jax.ShapeDtypeStruct((db * tnew, width), F32),
        compiler_params=_params(("arbitrary", "arbitrary")),
        name="decode_sample",
    )(page_table, q, kn, vn, cnt, umat, later, lf_t, ck_t, cv_t)


def _s5_kernel(u_ref, b_ref, c_ref, lpr_ref, lpi_ref, d_ref, *rest, has_init, n, half):
    if has_init:
        h0r_ref, h0i_ref, y_ref, hr_ref, hi_ref, hlr, hli = rest
    else:
        y_ref, hr_ref, hi_ref, hlr, hli, hpr, hpi, car, cai = rest
    nslab = u_ref.shape[0]
    hs = nslab // 2

    def u_rows(j):
        return [u_ref[s, pl.ds(j, n, stride=CHUNK), :] for s in range(nslab)]

    lr1, li1 = lpr_ref[0:1, :], lpi_ref[0:1, :]
    hr = hi = None
    for j in range(CHUNK):
        us = u_rows(j)
        ua = jnp.concatenate(us[:hs], axis=1).astype(BF16)
        ub = jnp.concatenate(us[hs:], axis=1).astype(BF16)
        ba = _dot(ua, b_ref[0])
        bb = _dot(ub, b_ref[1])
        bur = jnp.concatenate([ba[:, :half], bb[:, :half]], axis=1)
        bui = jnp.concatenate([ba[:, half:], bb[:, half:]], axis=1)
        if hr is None:
            hr, hi = bur, bui
        else:
            hr, hi = lr1 * hr - li1 * hi + bur, lr1 * hi + li1 * hr + bui
        hlr[j] = hr
        hli[j] = hi

    if has_init:
        pr_all, pi_all = h0r_ref[...], h0i_ref[...]
    else:
        i = pl.program_id(1)

        @pl.when(i == 0)
        def _():
            car[...] = jnp.zeros_like(car)
            cai[...] = jnp.zeros_like(cai)

        ltr, lti = lpr_ref[CHUNK - 1:CHUNK, :], lpi_ref[CHUNK - 1:CHUNK, :]

        def step(c, carry):
            pr, pi_ = carry
            hpr[pl.ds(c, 1), :] = pr
            hpi[pl.ds(c, 1), :] = pi_
            er = hlr[CHUNK - 1, pl.ds(c, 1), :]
            ei = hli[CHUNK - 1, pl.ds(c, 1), :]
            return ltr * pr - lti * pi_ + er, ltr * pi_ + lti * pr + ei

        pr, pi_ = lax.fori_loop(0, n, step, (car[...], cai[...]))
        car[...] = pr
        cai[...] = pi_
        hr_ref[0] = pr
        hi_ref[0] = pi_
        pr_all, pi_all = hpr[...], hpi[...]

    for j in range(CHUNK):
        lr, li = lpr_ref[j:j + 1, :], lpi_ref[j:j + 1, :]
        fr = hlr[j] + lr * pr_all - li * pi_all
        fi = hli[j] + lr * pi_all + li * pr_all
        if has_init and j == CHUNK - 1:
            hr_ref[...] = fr
            hi_ref[...] = fi
        la = jnp.concatenate([fr[:, :half], fi[:, :half]], axis=1).astype(BF16)
        lb = jnp.concatenate([fr[:, half:], fi[:, half:]], axis=1).astype(BF16)
        y = jnp.concatenate([_dot(la, c_ref[0]), _dot(lb, c_ref[1])], axis=1)
        us = u_rows(j)
        for s in range(nslab):
            y_ref[s, pl.ds(j, n, stride=CHUNK), :] = (
                y[:, s * LANES:(s + 1) * LANES] + d_ref[:, s * LANES:(s + 1) * LANES] * us[s])


def _s5(u_slabs, bmat, cmat, lpr, lpi, dskip, h0=None, *, nseq, tm):
    nslab, t, _ = u_slabs.shape
    l = t // nseq
    n = tm // CHUNK
    sw = lpr.shape[1]
    half = sw // 2
    has_init = h0 is not None
    consts = (bmat, cmat, lpr, lpi, dskip)
    scratch = [pltpu.VMEM((CHUNK, n, sw), F32), pltpu.VMEM((CHUNK, n, sw), F32)]
    if has_init:
        nb = t // tm
        grid = (1, nb)
        slab_spec = pl.BlockSpec((nslab, tm, LANES), lambda bi, i: (0, i, 0))
        st_spec = pl.BlockSpec((n, sw), lambda bi, i: (i, 0))
        ins = [u_slabs, *consts, h0[0], h0[1]]
        in_specs = [slab_spec] + [_const_spec(c.shape) for c in consts] + [st_spec, st_spec]
        st_shape = jax.ShapeDtypeStruct((t // CHUNK, sw), F32)
    else:
        nb = l // tm
        grid = (nseq, nb)
        slab_spec = pl.BlockSpec((nslab, tm, LANES), lambda bi, i: (0, bi * nb + i, 0))
        st_spec = pl.BlockSpec((1, 1, sw), lambda bi, i: (bi, 0, 0))
        ins = [u_slabs, *consts]
        in_specs = [slab_spec] + [_const_spec(c.shape) for c in consts]
        st_shape = jax.ShapeDtypeStruct((nseq, 1, sw), F32)
        scratch += [pltpu.VMEM((n, sw), F32), pltpu.VMEM((n, sw), F32),
                    pltpu.VMEM((1, sw), F32), pltpu.VMEM((1, sw), F32)]
    return pl.pallas_call(
        functools.partial(_s5_kernel, has_init=has_init, n=n, half=half),
        grid=grid, in_specs=in_specs,
        out_specs=[slab_spec, st_spec, st_spec],
        out_shape=[jax.ShapeDtypeStruct((nslab, t, LANES), F32), st_shape, st_shape],
        scratch_shapes=scratch,
        compiler_params=_params(("arbitrary", "arbitrary")),
        name="s5_sample" if has_init else "s5_prompt",
    )(*ins)


def _gelu_tanh(x):
    return x * (0.5 * (1.0 + jnp.tanh(math.sqrt(2.0 / math.pi) * (x + 0.044715 * (x * x * x)))))


def _tail_kernel(x_ref, att_ref, y_ref, wglu_ref, gatt_ref, gssm_ref, woa_ref, wos_ref, g2_ref,
                 wg_ref, wu_ref, wd_ref, gf_ref, o_ref, *, att_transposed, sw):
    x = x_ref[...]
    yv = jnp.concatenate([y_ref[s] for s in range(y_ref.shape[0])], axis=1)
    gv = _dot(_gelu_tanh(yv).astype(BF16), wglu_ref[...])
    ssm = gv[:, :sw] * jax.nn.sigmoid(gv[:, sw:])
    ssm_n = _rms_rows(ssm, gssm_ref[...]).astype(BF16)
    if att_transposed:
        at = att_ref[0]
        r = lax.rsqrt(jnp.mean(at * at, axis=0, keepdims=True) + EPS)
        att_n = (at * r * gatt_ref[...]).astype(BF16)
        ho = _dot_tn(att_n, woa_ref[...])
    else:
        ho = _dot(_rms_rows(att_ref[...], gatt_ref[...]).astype(BF16), woa_ref[...])
    h = x + ho + _dot(ssm_n, wos_ref[...])
    hn = _rms_rows(h, g2_ref[...]).astype(BF16)
    gt = _dot(hn, wg_ref[...])
    act = (gt * jax.nn.sigmoid(gt) * _dot(hn, wu_ref[...])).astype(BF16)
    y = h + _dot(act, wd_ref[...])
    o_ref[...] = _rms_rows(y, gf_ref[...])


def _tail(x, att, y_slabs, wglu, gatt, gssm, woa, wos, g2, wg, wu, wd, gf, *, att_transposed, tm):
    b, l, d = x.shape
    t = b * l
    nb = l // tm
    nslab = y_slabs.shape[0]
    sw = nslab * LANES
    tok = lambda bi, i: (bi * nb + i, 0)
    if att_transposed:
        att_spec = pl.BlockSpec((1, att.shape[1], tm), lambda bi, i: (bi, 0, i))
    else:
        att_spec = pl.BlockSpec((tm, att.shape[1]), tok)
    consts_a = (wglu, gatt, gssm, woa, wos, g2, wg, wu, wd, gf)
    out = pl.pallas_call(
        functools.partial(_tail_kernel, att_transposed=att_transposed, sw=sw),
        grid=(b, nb),
        in_specs=[pl.BlockSpec((tm, d), tok), att_spec,
                  pl.BlockSpec((nslab, tm, LANES), lambda bi, i: (0, bi * nb + i, 0))]
                 + [_const_spec(c.shape) for c in consts_a],
        out_specs=pl.BlockSpec((tm, d), tok),
        out_shape=jax.ShapeDtypeStruct((t, d), F32),
        compiler_params=_params(("arbitrary", "arbitrary")),
        name="tail_prompt" if att_transposed else "tail_sample",
    )(x.reshape(t, d), att, y_slabs, *consts_a)
    return out.reshape(b, l, d)


def _blockdiag(m):
    g, r, c = m.shape
    eye = jnp.eye(g, dtype=m.dtype)
    return (m[:, :, None, :] * eye[:, None, :, None]).reshape(g * r, g * c)


def _s5_params(a_re, a_im, log_dt, b_re, b_im, c_re, c_im, d):
    ar, ai = a_re.astype(F32), a_im.astype(F32)
    g, p = ar.shape
    dt = jnp.exp(log_dt.astype(F32))[:, None]
    mag = jnp.exp(ar * dt)
    lam_re = mag * jnp.cos(ai * dt)
    lam_im = mag * jnp.sin(ai * dt)
    den = ar * ar + ai * ai
    q_re = ((lam_re - 1.0) * ar + lam_im * ai) / den
    q_im = (lam_im * ar - (lam_re - 1.0) * ai) / den
    br, bi = b_re.astype(F32), b_im.astype(F32)
    bb_re = (q_re[..., None] * br - q_im[..., None] * bi).transpose(0, 2, 1)
    bb_im = (q_re[..., None] * bi + q_im[..., None] * br).transpose(0, 2, 1)
    gh = g // 2
    bmat = jnp.stack([jnp.concatenate([_blockdiag(bb_re[s]), _blockdiag(bb_im[s])], axis=1)
                      for s in (slice(0, gh), slice(gh, g))]).astype(BF16)
    cr = c_re.astype(F32).transpose(0, 2, 1)
    ci = c_im.astype(F32).transpose(0, 2, 1)
    cmat = jnp.stack([jnp.concatenate([_blockdiag(cr[s]), -_blockdiag(ci[s])], axis=0)
                      for s in (slice(0, gh), slice(gh, g))]).astype(BF16)
    pr, pi_ = [lam_re], [lam_im]
    for _ in range(CHUNK - 1):
        pr, pi_ = (pr + [pr[-1] * lam_re - pi_[-1] * lam_im], pi_ + [pr[-1] * lam_im + pi_[-1] * lam_re])
    lpr = jnp.stack(pr).reshape(CHUNK, g * p)
    lpi = jnp.stack(pi_).reshape(CHUNK, g * p)
    return bmat, cmat, lpr, lpi, d.astype(F32).reshape(1, -1)


def _proj_params(w_in, b_f, norm1_g, d):
    att = d // 2
    heads = att // HEAD_DIM
    scale = HEAD_DIM ** -0.5
    w = w_in.astype(F32)
    wq = w[:, :att] * scale
    wk, wv = w[:, att:2 * att], w[:, 2 * att:3 * att]
    wfl = w[:, 3 * att:3 * att + heads]
    wu = w[:, 3 * att + heads:]
    wf = jnp.zeros((d, LANES), F32).at[:, :3 * heads].set(jnp.tile(wfl, (1, 3))).astype(BF16)
    bf = jnp.zeros((1, LANES), F32).at[0, :3 * heads].set(jnp.tile(b_f.astype(F32), 3))
    wft = jnp.zeros((2 * heads, d), F32).at[:heads].set(wfl.T).astype(BF16)
    bft = jnp.zeros((2 * heads, 1), F32).at[:heads, 0].set(b_f.astype(F32))
    pk = np.zeros((LANES, att), np.float32)
    pq = np.zeros((att, 4 * heads), np.float32)
    for h in range(heads):
        base = h * HEAD_DIM
        for j in range(3):
            pk[3 * heads, base + j] = 1.0
            pk[j * heads + h, base + 3 + j] = 1.0
            pq[base + j, j * heads + h] = 1.0
            pq[base + 3 + j, 3 * heads] = -1.0
    return dict(
        g1=norm1_g.astype(F32).reshape(1, d),
        wn_prompt=jnp.concatenate([wk, wu], axis=1).astype(BF16),
        wn_sample=jnp.concatenate([wq, wk, wv, wu], axis=1).astype(BF16),
        wf=wf, bf=bf, wt=jnp.concatenate([wq * LOG2E, wk, wv], axis=1).T.astype(BF16), wft=wft, bft=bft,
        pk=jnp.asarray(pk, BF16), pq=jnp.asarray(pq, BF16))


def _layer(xp, xs, cache_k, cache_v, cache_logf, st_re, st_im, page_table, w):
    b, l, d = xp.shape
    db, tnew, _ = xs.shape
    att = d // 2
    heads = att // HEAD_DIM
    sw = d - att
    g = sw // SSM_GROUP
    tm, tq, tk, hpb = 512, 512, 256, 4
    tm_s5 = min(1024, l)
    pps = min(16, page_table.shape[1])

    pp = _proj_params(w["w_in"], w["b_f"], w["norm1_g"], d)
    bmat, cmat, lpr, lpi, dskip = _s5_params(w["a_re"], w["a_im"], w["log_dt"], w["b_re"], w["b_im"],
                                             w["c_re"], w["c_im"], w["d"])
    tail_w = (w["w_glu"].astype(BF16), None, w["norm_ssm_g"].astype(F32).reshape(1, sw),
              w["w_out"][:att].astype(BF16), w["w_out"][att:].astype(BF16),
              w["norm2_g"].astype(F32).reshape(1, d), w["w_gate"].astype(BF16), w["w_up"].astype(BF16),
              w["w_down"].astype(BF16), w["norm_f_g"].astype(F32).reshape(1, d))
    gatt = w["norm_attn_g"].astype(F32)
    pconst = (pp["g1"], None, pp["wf"], pp["bf"], pp["wt"], pp["wft"], pp["bft"], pp["pk"], pp["pq"])

    def consts(kind):
        c = list(pconst)
        c[1] = pp["wn_" + kind]
        return c

    up, kt, vt32, lft, ka, qta, vt = _proj(xp, *consts("prompt"), prompt=True, tm=tm, tk=tk)
    att_t = _fox(ka, qta, vt, tq=tq, tk=tk, hpb=hpb)
    yp_s, hrp, hip = _s5(up, bmat, cmat, lpr, lpi, dskip, nseq=b, tm=tm_s5)
    tw = list(tail_w)
    tw[1] = gatt.reshape(att, 1)
    y_prompt = _tail(xp, att_t, yp_s, *tw, att_transposed=True, tm=tm)

    ts = db * tnew
    tms = min(tm, ts)
    us, qs, ks, vs, lfs, cnt = _proj(xs.reshape(1, ts, d), *consts("sample"), prompt=False, tm=tms, tk=tk)
    cnt = cnt.reshape(heads, db, tnew).transpose(1, 0, 2)
    att_s = _decode(page_table, qs, ks, vs, cnt, cache_logf.transpose(0, 2, 1),
                    cache_k.transpose(0, 2, 3, 1), cache_v.transpose(0, 2, 3, 1), tnew=tnew, pps=pps)
    h0 = (st_re.astype(F32).reshape(db, g * STATE_DIM), st_im.astype(F32).reshape(db, g * STATE_DIM))
    ys_s, hrs, his = _s5(us, bmat, cmat, lpr, lpi, dskip, h0, nseq=db, tm=tms)
    tw[1] = gatt.reshape(1, att)
    y_sample = _tail(xs.reshape(1, ts, d), att_s, ys_s, *tw, att_transposed=False, tm=tms)

    return (y_prompt, y_sample.reshape(db, tnew, d),
            kt.transpose(0, 3, 1, 2)[None], vt32.transpose(0, 3, 1, 2)[None],
            lft.transpose(0, 2, 1)[None],
            hrp.reshape(1, b, g, STATE_DIM), hip.reshape(1, b, g, STATE_DIM),
            ks.reshape(1, db, tnew, heads, HEAD_DIM), vs.reshape(1, db, tnew, heads, HEAD_DIM),
            lfs.reshape(1, db, tnew, heads),
            hrs.reshape(1, db, g, STATE_DIM), his.reshape(1, db, g, STATE_DIM))


def kernel(x_prompt, x_sample, cache_k, cache_v, cache_logf, state_ssm_re, state_ssm_im, page_table, norm1_g, w_in, b_f, ssm_a_re, ssm_a_im, ssm_log_dt, ssm_b_re, ssm_b_im, ssm_c_re, ssm_c_im, ssm_d, w_glu, norm_attn_g, norm_ssm_g, w_out, norm2_g, w_gate, w_up, w_down, norm_f_g):
    assert w_in.shape[0] == 1, "single-layer trunk"
    w = dict(norm1_g=norm1_g[0], w_in=w_in[0], b_f=b_f[0], a_re=ssm_a_re[0], a_im=ssm_a_im[0],
             log_dt=ssm_log_dt[0], b_re=ssm_b_re[0], b_im=ssm_b_im[0], c_re=ssm_c_re[0], c_im=ssm_c_im[0],
             d=ssm_d[0], w_glu=w_glu[0], norm_attn_g=norm_attn_g[0], norm_ssm_g=norm_ssm_g[0],
             w_out=w_out[0], norm2_g=norm2_g[0], w_gate=w_gate[0], w_up=w_up[0], w_down=w_down[0],
             norm_f_g=norm_f_g)
    return _layer(x_prompt, x_sample, cache_k[0], cache_v[0], cache_logf[0], state_ssm_re[0],
                  state_ssm_im[0], page_table, w)
```

```python
import functools
import math

import jax
import jax.numpy as jnp
import numpy as np
from jax import lax
from jax.experimental import pallas as pl
from jax.experimental.pallas import tpu as pltpu

F32 = jnp.float32
BF16 = jnp.bfloat16

HEAD_DIM = 64
SSM_GROUP = 16
STATE_DIM = 64
EPS = 1e-6
NEG = -1e30
LANES = 128
CHUNK = 8
VT_ROWS = HEAD_DIM + 16
LOG2E = math.log2(math.e)
VMEM_LIMIT = 56 * 1024 * 1024


def _dot(a, b):
    return jnp.dot(a, b, preferred_element_type=F32)


def _dot_nt(a, b):
    return lax.dot_general(a, b, (((1,), (1,)), ((), ())), preferred_element_type=F32)


def _dot_tn(a, b):
    return lax.dot_general(a, b, (((0,), (0,)), ((), ())), preferred_element_type=F32)


def _split3(x):
    hi = x.astype(BF16)
    r1 = x - hi.astype(F32)
    mid = r1.astype(BF16)
    lo = (r1 - mid.astype(F32)).astype(BF16)
    return hi, mid, lo


def _sum3(x, n, axis):
    if axis == 0:
        return x[:n] + x[n:2 * n] + x[2 * n:3 * n]
    return x[:, :n] + x[:, n:2 * n] + x[:, 2 * n:3 * n]


def _rms_rows(x, g):
    r = lax.rsqrt(jnp.mean(x * x, axis=-1, keepdims=True) + EPS)
    return x * r * g


def _log_sigmoid(x):
    return jnp.minimum(x, 0.0) - jnp.log1p(jnp.exp(-jnp.abs(x)))


def _params(sem, vmem=VMEM_LIMIT):
    return pltpu.CompilerParams(dimension_semantics=sem, vmem_limit_bytes=vmem)


def _const_spec(shape):
    nd = len(shape)
    return pl.BlockSpec(shape, lambda *_: (0,) * nd, pipeline_mode=pl.Buffered(1))


def _proj_kernel(x_ref, g_ref, wn_ref, wf_ref, bf_ref, wt_ref, wft_ref, bft_ref, pk_ref, pq_ref,
                 *out_and_scratch, prompt, tm, att, heads, tk):
    if prompt:
        (u_ref, kt_ref, vt32_ref, lft_ref, ka_ref, qta_ref, vt_ref, cart_ref) = out_and_scratch
    else:
        (u_ref, q_ref, k_ref, v_ref, lf_ref, cnt_ref) = out_and_scratch
    i = pl.program_id(1)
    xn = _rms_rows(x_ref[...], g_ref[...]).astype(BF16)
    z = _dot(xn, wn_ref[...])
    if prompt:
        k, u = z[:, :att], z[:, att:]
    else:
        q, k, v, u = z[:, :att], z[:, att:2 * att], z[:, 2 * att:3 * att], z[:, 3 * att:]
        q_ref[...] = q
        k_ref[...] = k
        v_ref[...] = v
    for s in range(u.shape[1] // LANES):
        u_ref[s] = u[:, s * LANES:(s + 1) * LANES]
    lft = _log_sigmoid(_dot_nt(wft_ref[...], xn) + bft_ref[...])
    row = lax.broadcasted_iota(jnp.int32, (tm, tm), 0)
    col = lax.broadcasted_iota(jnp.int32, (tm, tm), 1)
    if prompt:
        triu = jnp.where(row <= col, 1.0, 0.0).astype(BF16)
    else:
        same = (row // CHUNK) == (col // CHUNK)
        triu = jnp.where(same & (row <= col), 1.0, 0.0).astype(BF16)
    hi, mid, lo = _split3(lft[:heads])
    zpad = jnp.zeros((heads, tm), BF16)
    ct = _sum3(_dot(jnp.concatenate([hi, mid, lo, zpad], axis=0), triu), heads, 0)
    if not prompt:
        lf_ref[...] = _log_sigmoid(_dot(xn, wf_ref[...]) + bf_ref[...])[:, :heads]
        cnt_ref[...] = ct
        return

    @pl.when(i == 0)
    def _():
        cart_ref[...] = jnp.zeros_like(cart_ref)

    lft_ref[0] = lft[:heads]
    ct = ct + cart_ref[:, 0:1]
    cart_ref[...] = jnp.broadcast_to(ct[:, tm - 1:tm], cart_ref.shape)
    zt = _dot_nt(wt_ref[...], xn)
    srow = lax.broadcasted_iota(jnp.int32, (4 * heads, tm), 0)
    thi, tmid, tlo = _split3(ct * LOG2E)
    ctpack = jnp.concatenate([thi, tmid, tlo, jnp.ones((heads, tm), BF16)], axis=0)
    ctpack = jnp.where(srow <= 3 * heads, ctpack, jnp.zeros_like(ctpack))
    qaug = _dot(pq_ref[...], ctpack)
    kaug = _dot_tn(ctpack, pk_ref[...])
    for h in range(heads):
        sl = slice(h * HEAD_DIM, (h + 1) * HEAD_DIM)
        kt = zt[att + h * HEAD_DIM:att + (h + 1) * HEAD_DIM, :]
        vt = zt[2 * att + h * HEAD_DIM:2 * att + (h + 1) * HEAD_DIM, :]
        kt_ref[0, h] = kt
        vt32_ref[0, h] = vt
        ka_ref[0, h, :, 0:HEAD_DIM] = k[:, sl].astype(BF16)
        ka_ref[0, h, :, HEAD_DIM:2 * HEAD_DIM] = kaug[:, sl].astype(BF16)
        qta_ref[0, h, 0:HEAD_DIM, :] = zt[sl, :].astype(BF16)
        qta_ref[0, h, HEAD_DIM:2 * HEAD_DIM, :] = qaug[sl, :].astype(BF16)
        for jj in range(tm // tk):
            vt_ref[0, h, jj, 0:HEAD_DIM, :] = vt[:, jj * tk:(jj + 1) * tk].astype(BF16)
            vt_ref[0, h, jj, HEAD_DIM:VT_ROWS, :] = jnp.ones((VT_ROWS - HEAD_DIM, tk), BF16)


def _proj(x, g1, wn, wf, bf, wt, wft, bft, pk, pq, *, prompt, tm, tk):
    b, l, d = x.shape
    att = d // 2
    heads = att // HEAD_DIM
    nslab = (d - att) // LANES
    t = b * l
    nb = l // tm
    x2 = x.reshape(t, d)
    tok = lambda bi, i: (bi * nb + i, 0)
    out_shape = [jax.ShapeDtypeStruct((nslab, t, LANES), F32)]
    out_specs = [pl.BlockSpec((nslab, tm, LANES), lambda bi, i: (0, bi * nb + i, 0))]
    scratch = []
    if prompt:
        hd_t = pl.BlockSpec((1, heads, HEAD_DIM, tm), lambda bi, i: (bi, 0, 0, i))
        out_shape += [jax.ShapeDtypeStruct((b, heads, HEAD_DIM, l), F32),
                      jax.ShapeDtypeStruct((b, heads, HEAD_DIM, l), F32),
                      jax.ShapeDtypeStruct((b, heads, l), F32),
                      jax.ShapeDtypeStruct((b, heads, l, 2 * HEAD_DIM), BF16),
                      jax.ShapeDtypeStruct((b, heads, 2 * HEAD_DIM, l), BF16),
                      jax.ShapeDtypeStruct((b, heads, l // tk, VT_ROWS, tk), BF16)]
        out_specs += [hd_t, hd_t,
                      pl.BlockSpec((1, heads, tm), lambda bi, i: (bi, 0, i)),
                      pl.BlockSpec((1, heads, tm, 2 * HEAD_DIM), lambda bi, i: (bi, 0, i, 0)),
                      pl.BlockSpec((1, heads, 2 * HEAD_DIM, tm), lambda bi, i: (bi, 0, 0, i)),
                      pl.BlockSpec((1, heads, tm // tk, VT_ROWS, tk), lambda bi, i: (bi, 0, i, 0, 0))]
        scratch = [pltpu.VMEM((heads, LANES), F32)]
    else:
        out_shape += [jax.ShapeDtypeStruct((t, att), F32)] * 3 + [
            jax.ShapeDtypeStruct((t, heads), F32), jax.ShapeDtypeStruct((heads, t), F32)]
        out_specs += [pl.BlockSpec((tm, att), tok)] * 3 + [
            pl.BlockSpec((tm, heads), tok), pl.BlockSpec((heads, tm), lambda bi, i: (0, bi * nb + i))]
    consts = (g1, wn, wf, bf, wt, wft, bft, pk, pq)
    return pl.pallas_call(
        functools.partial(_proj_kernel, prompt=prompt, tm=tm, att=att, heads=heads, tk=tk),
        grid=(b, nb),
        in_specs=[pl.BlockSpec((tm, d), tok)] + [_const_spec(c.shape) for c in consts],
        out_specs=out_specs, out_shape=out_shape, scratch_shapes=scratch,
        compiler_params=_params(("arbitrary", "arbitrary")),
        name="proj_prompt" if prompt else "proj_sample",
    )(x2, *consts)


def _fox_kernel(ka_ref, qta_ref, vt_ref, o_ref, s0_ref, s1_ref, *, tq, tk, hpb):
    assert tq == 2 * tk
    qi = pl.program_id(2)
    qts = [qta_ref[0, hh] for hh in range(hpb)]

    def produce(j, buf):
        for hh in range(hpb):
            buf[hh] = _dot(ka_ref[0, hh, j], qts[hh])

    def consume(j, buf, carries, masked):
        ps, ms, alphas = [], [], []
        for hh in range(hpb):
            s = buf[hh] if masked is None else jnp.where(masked, buf[hh], NEG)
            m = carries[hh][0]
            m_new = jnp.maximum(m, jnp.max(s, axis=0, keepdims=True))
            ps.append(jnp.exp2(s - m_new).astype(BF16))
            alphas.append(jnp.exp2(m - m_new))
            ms.append(m_new)
        return tuple((ms[hh], alphas[hh] * carries[hh][1] + _dot(vt_ref[0, hh, j], ps[hh]))
                     for hh in range(hpb))

    def pair(pp, carries):
        j = 2 * pp
        produce(j + 1, s1_ref)
        carries = consume(j, s0_ref, carries, None)
        produce(j + 2, s0_ref)
        return consume(j + 1, s1_ref, carries, None)

    def quad(qq, carries):
        return pair(2 * qq + 1, pair(2 * qq, carries))

    init = tuple((jnp.full((1, tq), NEG, F32), jnp.zeros((VT_ROWS, tq), F32)) for _ in range(hpb))
    produce(0, s0_ref)
    carries = lax.fori_loop(0, qi // 2, quad, init)
    carries = lax.fori_loop(2 * (qi // 2), qi, pair, carries)
    key = lax.broadcasted_iota(jnp.int32, (tk, tq), 0)
    qry = lax.broadcasted_iota(jnp.int32, (tk, tq), 1)
    produce(2 * qi + 1, s1_ref)
    carries = consume(2 * qi, s0_ref, carries, key <= qry)
    carries = consume(2 * qi + 1, s1_ref, carries, key + tk <= qry)
    for hh in range(hpb):
        acc = carries[hh][1]
        o_ref[0, hh * HEAD_DIM:(hh + 1) * HEAD_DIM, :] = acc[:HEAD_DIM] / acc[HEAD_DIM:HEAD_DIM + 1]


def _fox(ka, qta, vt, *, tq, tk, hpb):
    b, heads, l, _ = ka.shape
    ka5 = ka.reshape(b, heads, l // tk, tk, 2 * HEAD_DIM)
    return pl.pallas_call(
        functools.partial(_fox_kernel, tq=tq, tk=tk, hpb=hpb),
        grid=(b, heads // hpb, l // tq),
        in_specs=[pl.BlockSpec((1, hpb, l // tk, tk, 2 * HEAD_DIM), lambda bi, h, i: (bi, h, 0, 0, 0),
                               pipeline_mode=pl.Buffered(1)),
                  pl.BlockSpec((1, hpb, 2 * HEAD_DIM, tq), lambda bi, h, i: (bi, h, 0, i)),
                  pl.BlockSpec((1, hpb, l // tk, VT_ROWS, tk), lambda bi, h, i: (bi, h, 0, 0, 0),
                               pipeline_mode=pl.Buffered(1))],
        out_specs=pl.BlockSpec((1, hpb * HEAD_DIM, tq), lambda bi, h, i: (bi, h, i)),
        out_shape=jax.ShapeDtypeStruct((b, heads * HEAD_DIM, l), F32),
        scratch_shapes=[pltpu.VMEM((hpb, tk, tq), F32), pltpu.VMEM((hpb, tk, tq), F32)],
        compiler_params=_params(("arbitrary", "arbitrary", "arbitrary")),
        name="fox_prompt",
    )(ka5, qta, vt)


def _past_bias(x, umat, later):
    rows = x.shape[0]
    hi, mid, lo = _split3(x)
    both = _sum3(_dot(jnp.concatenate([hi, mid, lo], axis=0), umat), rows, 0)
    rin = both[:, :LANES]
    tot = both[:, LANES:]
    hi, mid, lo = _split3(tot)
    return rin + _sum3(_dot(later, jnp.concatenate([hi, mid, lo], axis=1)), LANES, 1)


def _past_bias_consts(npages, heads, page):
    s = np.arange(page)
    umat = np.concatenate([s[:, None] > s[None, :], np.ones((page, page), bool)], axis=1)
    r = np.arange(npages * heads)
    later = (r[None, :] // heads > r[:, None] // heads) & (r[None, :] % heads == r[:, None] % heads)
    return jnp.asarray(umat, BF16), jnp.asarray(later, BF16)


def _decode_kernel(pt_ref, q_ref, kn_ref, vn_ref, cnt_ref, umat_ref, later_ref, lf_hbm, ck_hbm, cv_hbm, o_ref,
                   kbuf, vbuf, sems, xbuf, xsems, bias_ref, qbd_ref, m_ref, l_ref, acc_ref,
                   *, pps, npages, heads, tnew):
    b = pl.program_id(0)
    s = pl.program_id(1)
    nb = pl.num_programs(0)
    ns = pl.num_programs(1)
    step = b * ns + s
    slot = lax.rem(step, 2)
    bslot = lax.rem(b, 2)
    rows = heads * tnew
    width = heads * HEAD_DIM
    page = kbuf.shape[-1]
    rhead = lax.broadcasted_iota(jnp.int32, (rows, width), 0) // tnew
    chead = lax.broadcasted_iota(jnp.int32, (rows, width), 1) // HEAD_DIM

    def page_copy(bb, ss, sl, i, kv):
        src, dst = (ck_hbm, kbuf) if kv == 0 else (cv_hbm, vbuf)
        return pltpu.make_async_copy(src.at[pt_ref[bb, ss * pps + i]], dst.at[sl, i], sems.at[sl, kv, i])

    def start_step(bb, ss, sl):
        for i in range(pps):
            for kv in range(2):
                page_copy(bb, ss, sl, i, kv).start(priority=kv)

    def logf_copy(bb, sl, p):
        return pltpu.make_async_copy(lf_hbm.at[pt_ref[bb, p]], xbuf.at[sl, pl.ds(p * heads, heads)],
                                     xsems.at[sl, p])

    @pl.when(step == 0)
    def _():
        for p in range(npages):
            logf_copy(0, 0, p).start()
        start_step(0, 0, 0)

    @pl.when((s == 0) & (b + 1 < nb))
    def _():
        for p in range(npages):
            logf_copy(b + 1, 1 - bslot, p).start()

    @pl.when(step + 1 < nb * ns)
    def _():
        wrap = s + 1 == ns
        start_step(jnp.where(wrap, b + 1, b), jnp.where(wrap, 0, s + 1), 1 - slot)

    @pl.when(s == 0)
    def _():
        qtile = jnp.concatenate([q_ref[...]] * heads, axis=0)
        qbd_ref[...] = jnp.where(rhead == chead, qtile, 0.0).astype(BF16)
        m_ref[...] = jnp.full_like(m_ref, NEG)
        l_ref[...] = jnp.zeros_like(l_ref)
        acc_ref[...] = jnp.zeros_like(acc_ref)
        for p in range(npages):
            logf_copy(b, bslot, p).wait()
        bias_ref[...] = _past_bias(xbuf[bslot], umat_ref[...], later_ref[...])

    qbd = qbd_ref[...]

    def wait_pages(kv):
        for i in range(pps):
            page_copy(b, s, slot, i, kv).wait()

    def cat_pages(buf, kv, g):
        return jnp.concatenate([buf[slot, 2 * g + e].reshape(width, page).astype(BF16) for e in range(2)], axis=1)

    wait_pages(0)
    scores = []
    for g in range(pps // 2):
        bias = jnp.concatenate(
            [jnp.concatenate([jnp.broadcast_to(bias_ref[pl.ds((s * pps + 2 * g + e) * heads + h, 1), :],
                                               (tnew, page))
                              for h in range(heads)], axis=0) for e in range(2)], axis=1)
        scores.append(_dot(qbd, cat_pages(kbuf, 0, g)) + bias)
    m_old = m_ref[...]
    m_new = m_old
    for st in scores:
        m_new = jnp.maximum(m_new, jnp.max(st, axis=1, keepdims=True))
    alpha = jnp.exp(m_old - m_new)
    l = alpha * l_ref[...]
    acc = alpha * acc_ref[...]
    wait_pages(1)
    for g in range(pps // 2):
        p = jnp.exp(scores[g] - m_new)
        l = l + jnp.sum(p, axis=1, keepdims=True)
        acc = acc + _dot_nt(p.astype(BF16), cat_pages(vbuf, 1, g))
    m_ref[...] = m_new
    l_ref[...] = l
    acc_ref[...] = acc

    @pl.when(s == ns - 1)
    def _():
        st = _dot_nt(qbd, kn_ref[...].astype(BF16))
        bias = jnp.concatenate(
            [jnp.broadcast_to(cnt_ref[0, h:h + 1, :], (tnew, tnew)) for h in range(heads)], axis=0)
        qi = lax.broadcasted_iota(jnp.int32, (rows, tnew), 0) % tnew
        kj = lax.broadcasted_iota(jnp.int32, (rows, tnew), 1)
        st = jnp.where(kj <= qi, st - bias, NEG)
        m_fin = jnp.maximum(m_new, jnp.max(st, axis=1, keepdims=True))
        a2 = jnp.exp(m_new - m_fin)
        p = jnp.exp(st - m_fin)
        lf = a2 * l + jnp.sum(p, axis=1, keepdims=True)
        accf = a2 * acc + _dot(p.astype(BF16), vn_ref[...].astype(BF16))
        res = jnp.where(rhead == chead, accf / lf, 0.0)
        out = res[0:tnew]
        for h in range(1, heads):
            out = out + res[h * tnew:(h + 1) * tnew]
        o_ref[...] = out


def _decode(page_table, q, kn, vn, cnt, lf_t, ck_t, cv_t, *, tnew, pps):
    n_pool, heads, hd, page = ck_t.shape
    db, npages = page_table.shape
    width = heads * hd
    rows = heads * tnew
    assert pps % 2 == 0 and npages % pps == 0 and page == LANES
    umat, later = _past_bias_consts(npages, heads, page)
    tokspec = pl.BlockSpec((tnew, width), lambda b, s, pt: (b, 0))
    pages = pltpu.VMEM((2, pps, heads, hd, page), F32)
    grid_spec = pltpu.PrefetchScalarGridSpec(
        num_scalar_prefetch=1, grid=(db, npages // pps),
        in_specs=[tokspec, tokspec, tokspec,
                  pl.BlockSpec((1, heads, tnew), lambda b, s, pt: (b, 0, 0)),
                  pl.BlockSpec(umat.shape, lambda b, s, pt: (0, 0)),
                  pl.BlockSpec(later.shape, lambda b, s, pt: (0, 0)),
                  pl.BlockSpec(memory_space=pl.ANY), pl.BlockSpec(memory_space=pl.ANY),
                  pl.BlockSpec(memory_space=pl.ANY)],
        out_specs=tokspec,
        scratch_shapes=[pages, pages, pltpu.SemaphoreType.DMA((2, 2, pps)),
                        pltpu.VMEM((2, npages * heads, page), F32), pltpu.SemaphoreType.DMA((2, npages)),
                        pltpu.VMEM((npages * heads, page), F32),
                        pltpu.VMEM((rows, width), BF16), pltpu.VMEM((rows, 1), F32),
                        pltpu.VMEM((rows, 1), F32), pltpu.VMEM((rows, width), F32)])
    return pl.pallas_call(
        functools.partial(_decode_kernel, pps=pps, npages=npages, heads=heads, tnew=tnew),
        grid_spec=grid_spec,
        out_shape=jax.ShapeDtypeStruct((db * tnew, width), F32),
        compiler_params=_params(("arbitrary", "arbitrary")),
        name="decode_sample",
    )(page_table, q, kn, vn, cnt, umat, later, lf_t, ck_t, cv_t)


def _s5_kernel(u_ref, b_ref, c_ref, lpr_ref, lpi_ref, d_ref, *rest, has_init, n, half):
    if has_init:
        h0r_ref, h0i_ref, y_ref, hr_ref, hi_ref, hlr, hli = rest
    else:
        y_ref, hr_ref, hi_ref, hlr, hli, hpr, hpi, car, cai = rest
    nslab = u_ref.shape[0]
    hs = nslab // 2

    def u_rows(j):
        return [u_ref[s, pl.ds(j, n, stride=CHUNK), :] for s in range(nslab)]

    lr1, li1 = lpr_ref[0:1, :], lpi_ref[0:1, :]
    hr = hi = None
    for j in range(CHUNK):
        us = u_rows(j)
        ua = jnp.concatenate(us[:hs], axis=1).astype(BF16)
        ub = jnp.concatenate(us[hs:], axis=1).astype(BF16)
        ba = _dot(ua, b_ref[0])
        bb = _dot(ub, b_ref[1])
        bur = jnp.concatenate([ba[:, :half], bb[:, :half]], axis=1)
        bui = jnp.concatenate([ba[:, half:], bb[:, half:]], axis=1)
        if hr is None:
            hr, hi = bur, bui
        else:
            hr, hi = lr1 * hr - li1 * hi + bur, lr1 * hi + li1 * hr + bui
        hlr[j] = hr
        hli[j] = hi

    if has_init:
        pr_all, pi_all = h0r_ref[...], h0i_ref[...]
    else:
        i = pl.program_id(1)

        @pl.when(i == 0)
        def _():
            car[...] = jnp.zeros_like(car)
            cai[...] = jnp.zeros_like(cai)

        ltr, lti = lpr_ref[CHUNK - 1:CHUNK, :], lpi_ref[CHUNK - 1:CHUNK, :]

        def step(c, carry):
            pr, pi_ = carry
            hpr[pl.ds(c, 1), :] = pr
            hpi[pl.ds(c, 1), :] = pi_
            er = hlr[CHUNK - 1, pl.ds(c, 1), :]
            ei = hli[CHUNK - 1, pl.ds(c, 1), :]
            return ltr * pr - lti * pi_ + er, ltr * pi_ + lti * pr + ei

        pr, pi_ = lax.fori_loop(0, n, step, (car[...], cai[...]))
        car[...] = pr
        cai[...] = pi_
        hr_ref[0] = pr
        hi_ref[0] = pi_
        pr_all, pi_all = hpr[...], hpi[...]

    for j in range(CHUNK):
        lr, li = lpr_ref[j:j + 1, :], lpi_ref[j:j + 1, :]
        fr = hlr[j] + lr * pr_all - li * pi_all
        fi = hli[j] + lr * pi_all + li * pr_all
        if has_init and j == CHUNK - 1:
            hr_ref[...] = fr
            hi_ref[...] = fi
        la = jnp.concatenate([fr[:, :half], fi[:, :half]], axis=1).astype(BF16)
        lb = jnp.concatenate([fr[:, half:], fi[:, half:]], axis=1).astype(BF16)
        y = jnp.concatenate([_dot(la, c_ref[0]), _dot(lb, c_ref[1])], axis=1)
        us = u_rows(j)
        for s in range(nslab):
            y_ref[s, pl.ds(j, n, stride=CHUNK), :] = (
                y[:, s * LANES:(s + 1) * LANES] + d_ref[:, s * LANES:(s + 1) * LANES] * us[s])


def _s5(u_slabs, bmat, cmat, lpr, lpi, dskip, h0=None, *, nseq, tm):
    nslab, t, _ = u_slabs.shape
    l = t // nseq
    n = tm // CHUNK
    sw = lpr.shape[1]
    half = sw // 2
    has_init = h0 is not None
    consts = (bmat, cmat, lpr, lpi, dskip)
    scratch = [pltpu.VMEM((CHUNK, n, sw), F32), pltpu.VMEM((CHUNK, n, sw), F32)]
    if has_init:
        nb = t // tm
        grid = (1, nb)
        slab_spec = pl.BlockSpec((nslab, tm, LANES), lambda bi, i: (0, i, 0))
        st_spec = pl.BlockSpec((n, sw), lambda bi, i: (i, 0))
        ins = [u_slabs, *consts, h0[0], h0[1]]
        in_specs = [slab_spec] + [_const_spec(c.shape) for c in consts] + [st_spec, st_spec]
        st_shape = jax.ShapeDtypeStruct((t // CHUNK, sw), F32)
    else:
        nb = l // tm
        grid = (nseq, nb)
        slab_spec = pl.BlockSpec((nslab, tm, LANES), lambda bi, i: (0, bi * nb + i, 0))
        st_spec = pl.BlockSpec((1, 1, sw), lambda bi, i: (bi, 0, 0))
        ins = [u_slabs, *consts]
        in_specs = [slab_spec] + [_const_spec(c.shape) for c in consts]
        st_shape = jax.ShapeDtypeStruct((nseq, 1, sw), F32)
        scratch += [pltpu.VMEM((n, sw), F32), pltpu.VMEM((n, sw), F32),
                    pltpu.VMEM((1, sw), F32), pltpu.VMEM((1, sw), F32)]
    return pl.pallas_call(
        functools.partial(_s5_kernel, has_init=has_init, n=n, half=half),
        grid=grid, in_specs=in_specs,
        out_specs=[slab_spec, st_spec, st_spec],
        out_shape=[jax.ShapeDtypeStruct((nslab, t, LANES), F32), st_shape, st_shape],
        scratch_shapes=scratch,
        compiler_params=_params(("arbitrary", "arbitrary")),
        name="s5_sample" if has_init else "s5_prompt",
    )(*ins)


def _gelu_tanh(x):
    return x * (0.5 * (1.0 + jnp.tanh(math.sqrt(2.0 / math.pi) * (x + 0.044715 * (x * x * x)))))


def _tail_kernel(x_ref, att_ref, y_ref, wglu_ref, gatt_ref, gssm_ref, woa_ref, wos_ref, g2_ref,
                 wg_ref, wu_ref, wd_ref, gf_ref, o_ref, *, att_transposed, sw):
    x = x_ref[...]
    yv = jnp.concatenate([y_ref[s] for s in range(y_ref.shape[0])], axis=1)
    gv = _dot(_gelu_tanh(yv).astype(BF16), wglu_ref[...])
    ssm = gv[:, :sw] * jax.nn.sigmoid(gv[:, sw:])
    ssm_n = _rms_rows(ssm, gssm_ref[...]).astype(BF16)
    if att_transposed:
        at = att_ref[0]
        r = lax.rsqrt(jnp.mean(at * at, axis=0, keepdims=True) + EPS)
        att_n = (at * r * gatt_ref[...]).astype(BF16)
        ho = _dot_tn(att_n, woa_ref[...])
    else:
        ho = _dot(_rms_rows(att_ref[...], gatt_ref[...]).astype(BF16), woa_ref[...])
    h = x + ho + _dot(ssm_n, wos_ref[...])
    hn = _rms_rows(h, g2_ref[...]).astype(BF16)
    gt = _dot(hn, wg_ref[...])
    act = (gt * jax.nn.sigmoid(gt) * _dot(hn, wu_ref[...])).astype(BF16)
    y = h + _dot(act, wd_ref[...])
    o_ref[...] = _rms_rows(y, gf_ref[...])


def _tail(x, att, y_slabs, wglu, gatt, gssm, woa, wos, g2, wg, wu, wd, gf, *, att_transposed, tm):
    b, l, d = x.shape
    t = b * l
    nb = l // tm
    nslab = y_slabs.shape[0]
    sw = nslab * LANES
    tok = lambda bi, i: (bi * nb + i, 0)
    if att_transposed:
        att_spec = pl.BlockSpec((1, att.shape[1], tm), lambda bi, i: (bi, 0, i))
    else:
        att_spec = pl.BlockSpec((tm, att.shape[1]), tok)
    consts_a = (wglu, gatt, gssm, woa, wos, g2, wg, wu, wd, gf)
    out = pl.pallas_call(
        functools.partial(_tail_kernel, att_transposed=att_transposed, sw=sw),
        grid=(b, nb),
        in_specs=[pl.BlockSpec((tm, d), tok), att_spec,
                  pl.BlockSpec((nslab, tm, LANES), lambda bi, i: (0, bi * nb + i, 0))]
                 + [_const_spec(c.shape) for c in consts_a],
        out_specs=pl.BlockSpec((tm, d), tok),
        out_shape=jax.ShapeDtypeStruct((t, d), F32),
        compiler_params=_params(("arbitrary", "arbitrary")),
        name="tail_prompt" if att_transposed else "tail_sample",
    )(x.reshape(t, d), att, y_slabs, *consts_a)
    return out.reshape(b, l, d)


def _blockdiag(m):
    g, r, c = m.shape
    eye = jnp.eye(g, dtype=m.dtype)
    return (m[:, :, None, :] * eye[:, None, :, None]).reshape(g * r, g * c)


def _s5_params(a_re, a_im, log_dt, b_re, b_im, c_re, c_im, d):
    ar, ai = a_re.astype(F32), a_im.astype(F32)
    g, p = ar.shape
    dt = jnp.exp(log_dt.astype(F32))[:, None]
    mag = jnp.exp(ar * dt)
    lam_re = mag * jnp.cos(ai * dt)
    lam_im = mag * jnp.sin(ai * dt)
    den = ar * ar + ai * ai
    q_re = ((lam_re - 1.0) * ar + lam_im * ai) / den
    q_im = (lam_im * ar - (lam_re - 1.0) * ai) / den
    br, bi = b_re.astype(F32), b_im.astype(F32)
    bb_re = (q_re[..., None] * br - q_im[..., None] * bi).transpose(0, 2, 1)
    bb_im = (q_re[..., None] * bi + q_im[..., None] * br).transpose(0, 2, 1)
    gh = g // 2
    bmat = jnp.stack([jnp.concatenate([_blockdiag(bb_re[s]), _blockdiag(bb_im[s])], axis=1)
                      for s in (slice(0, gh), slice(gh, g))]).astype(BF16)
    cr = c_re.astype(F32).transpose(0, 2, 1)
    ci = c_im.astype(F32).transpose(0, 2, 1)
    cmat = jnp.stack([jnp.concatenate([_blockdiag(cr[s]), -_blockdiag(ci[s])], axis=0)
                      for s in (slice(0, gh), slice(gh, g))]).astype(BF16)
    pr, pi_ = [lam_re], [lam_im]
    for _ in range(CHUNK - 1):
        pr, pi_ = (pr + [pr[-1] * lam_re - pi_[-1] * lam_im], pi_ + [pr[-1] * lam_im + pi_[-1] * lam_re])
    lpr = jnp.stack(pr).reshape(CHUNK, g * p)
    lpi = jnp.stack(pi_).reshape(CHUNK, g * p)
    return bmat, cmat, lpr, lpi, d.astype(F32).reshape(1, -1)


def _proj_params(w_in, b_f, norm1_g, d):
    att = d // 2
    heads = att // HEAD_DIM
    scale = HEAD_DIM ** -0.5
    w = w_in.astype(F32)
    wq = w[:, :att] * scale
    wk, wv = w[:, att:2 * att], w[:, 2 * att:3 * att]
    wfl = w[:, 3 * att:3 * att + heads]
    wu = w[:, 3 * att + heads:]
    wf = jnp.zeros((d, LANES), F32).at[:, :heads].set(wfl).astype(BF16)
    bf = jnp.zeros((1, LANES), F32).at[0, :heads].set(b_f.astype(F32))
    wft = jnp.zeros((2 * heads, d), F32).at[:heads].set(wfl.T).astype(BF16)
    bft = jnp.zeros((2 * heads, 1), F32).at[:heads, 0].set(b_f.astype(F32))
    pk = np.zeros((4 * heads, att), np.float32)
    pq = np.zeros((att, 4 * heads), np.float32)
    for h in range(heads):
        base = h * HEAD_DIM
        for j in range(3):
            pk[3 * heads, base + j] = 1.0
            pk[j * heads + h, base + 3 + j] = 1.0
            pq[base + j, j * heads + h] = 1.0
            pq[base + 3 + j, 3 * heads] = -1.0
    return dict(
        g1=norm1_g.astype(F32).reshape(1, d),
        wn_prompt=jnp.concatenate([wk, wu], axis=1).astype(BF16),
        wn_sample=jnp.concatenate([wq, wk, wv, wu], axis=1).astype(BF16),
        wf=wf, bf=bf, wt=jnp.concatenate([wq * LOG2E, wk, wv], axis=1).T.astype(BF16), wft=wft, bft=bft,
        pk=jnp.asarray(pk, BF16), pq=jnp.asarray(pq, BF16))


def _layer(xp, xs, cache_k, cache_v, cache_logf, st_re, st_im, page_table, w):
    b, l, d = xp.shape
    db, tnew, _ = xs.shape
    att = d // 2
    heads = att // HEAD_DIM
    sw = d - att
    g = sw // SSM_GROUP
    tm, tq, tk, hpb = 512, 512, 256, 4
    tm_s5 = min(1024, l)
    pps = min(16, page_table.shape[1])

    pp = _proj_params(w["w_in"], w["b_f"], w["norm1_g"], d)
    bmat, cmat, lpr, lpi, dskip = _s5_params(w["a_re"], w["a_im"], w["log_dt"], w["b_re"], w["b_im"],
                                             w["c_re"], w["c_im"], w["d"])
    tail_w = (w["w_glu"].astype(BF16), None, w["norm_ssm_g"].astype(F32).reshape(1, sw),
              w["w_out"][:att].astype(BF16), w["w_out"][att:].astype(BF16),
              w["norm2_g"].astype(F32).reshape(1, d), w["w_gate"].astype(BF16), w["w_up"].astype(BF16),
              w["w_down"].astype(BF16), w["norm_f_g"].astype(F32).reshape(1, d))
    gatt = w["norm_attn_g"].astype(F32)
    pconst = (pp["g1"], None, pp["wf"], pp["bf"], pp["wt"], pp["wft"], pp["bft"], pp["pk"], pp["pq"])

    def consts(kind):
        c = list(pconst)
        c[1] = pp["wn_" + kind]
        return c

    up, kt, vt32, lft, ka, qta, vt = _proj(xp, *consts("prompt"), prompt=True, tm=tm, tk=tk)
    att_t = _fox(ka, qta, vt, tq=tq, tk=tk, hpb=hpb)
    yp_s, hrp, hip = _s5(up, bmat, cmat, lpr, lpi, dskip, nseq=b, tm=tm_s5)
    tw = list(tail_w)
    tw[1] = gatt.reshape(att, 1)
    y_prompt = _tail(xp, att_t, yp_s, *tw, att_transposed=True, tm=tm)

    ts = db * tnew
    tms = min(tm, ts)
    us, qs, ks, vs, lfs, cnt = _proj(xs.reshape(1, ts, d), *consts("sample"), prompt=False, tm=tms, tk=tk)
    cnt = cnt.reshape(heads, db, tnew).transpose(1, 0, 2)
    att_s = _decode(page_table, qs, ks, vs, cnt, cache_logf.transpose(0, 2, 1),
                    cache_k.transpose(0, 2, 3, 1), cache_v.transpose(0, 2, 3, 1), tnew=tnew, pps=pps)
    h0 = (st_re.astype(F32).reshape(db, g * STATE_DIM), st_im.astype(F32).reshape(db, g * STATE_DIM))
    ys_s, hrs, his = _s5(us, bmat, cmat, lpr, lpi, dskip, h0, nseq=db, tm=tms)
    tw[1] = gatt.reshape(1, att)
    y_sample = _tail(xs.reshape(1, ts, d), att_s, ys_s, *tw, att_transposed=False, tm=tms)

    return (y_prompt, y_sample.reshape(db, tnew, d),
            kt.transpose(0, 3, 1, 2)[None], vt32.transpose(0, 3, 1, 2)[None],
            lft.transpose(0, 2, 1)[None],
            hrp.reshape(1, b, g, STATE_DIM), hip.reshape(1, b, g, STATE_DIM),
            ks.reshape(1, db, tnew, heads, HEAD_DIM), vs.reshape(1, db, tnew, heads, HEAD_DIM),
            lfs.reshape(1, db, tnew, heads),
            hrs.reshape(1, db, g, STATE_DIM), his.reshape(1, db, g, STATE_DIM))


def kernel(x_prompt, x_sample, cache_k, cache_v, cache_logf, state_ssm_re, state_ssm_im, page_table, norm1_g, w_in, b_f, ssm_a_re, ssm_a_im, ssm_log_dt, ssm_b_re, ssm_b_im, ssm_c_re, ssm_c_im, ssm_d, w_glu, norm_attn_g, norm_ssm_g, w_out, norm2_g, w_gate, w_up, w_down, norm_f_g):
    assert w_in.shape[0] == 1, "single-layer trunk"
    w = dict(norm1_g=norm1_g[0], w_in=w_in[0], b_f=b_f[0], a_re=ssm_a_re[0], a_im=ssm_a_im[0],
             log_dt=ssm_log_dt[0], b_re=ssm_b_re[0], b_im=ssm_b_im[0], c_re=ssm_c_re[0], c_im=ssm_c_im[0],
             d=ssm_d[0], w_glu=w_glu[0], norm_attn_g=norm_attn_g[0], norm_ssm_g=norm_ssm_g[0],
             w_out=w_out[0], norm2_g=norm2_g[0], w_gate=w_gate[0], w_up=w_up[0], w_down=w_down[0],
             norm_f_g=norm_f_g)
    return _layer(x_prompt, x_sample, cache_k[0], cache_v[0], cache_logf[0], state_ssm_re[0],
                  state_ssm_im[0], page_table, w)
```
